```python
import math
import jax, jax.numpy as jnp
from jax import lax
import numpy as np

D_MODEL = 1024
BATCH = 8
SEQ = 2048
DEPTH = 2
DEC_BATCH = 32
DEC_SEQ = 4
PAST_LEN = 16384
PAGE_SIZE = 128

HEAD_DIM = 64
N_HEADS = D_MODEL // (2 * HEAD_DIM)
V_DIM = 2 * HEAD_DIM
ROPE_DIM = HEAD_DIM // 4
ROPE_THETA = 500000.0
Q_BLOCK = 128
QK_WIDTH = N_HEADS * 2 * HEAD_DIM
ATT_WIDTH = N_HEADS * V_DIM
SC_WIDTH = D_MODEL // 2
SC_KERNEL = 3
IN_WIDTH = 2 * QK_WIDTH + ATT_WIDTH + 3 * SC_WIDTH
OUT_WIDTH = ATT_WIDTH + SC_WIDTH
CONF_DIM = D_MODEL
CONF_KERNEL = 31
N_EXPERTS = 32
TOP_K = 4
D_FF = D_MODEL
SWIGLU_LIMIT = 7.0
SWIGLU_ALPHA = 1.702
MOE_BLOCK = 128
EPS = 1e-6
N_EVEN = (DEPTH + 1) // 2
N_ODD = DEPTH // 2

kernel_name = 'hybrid_diffattn_shortconv_conformer_moe_step'


def rmsnorm(x, g):
    xf = x.astype(jnp.float32)
    y = xf * lax.rsqrt(jnp.mean(jnp.square(xf), axis=-1, keepdims=True) + EPS)
    return (y * g.astype(jnp.float32)).astype(x.dtype)


def layernorm(x, g, b):
    xf = x.astype(jnp.float32)
    mu = jnp.mean(xf, axis=-1, keepdims=True)
    var = jnp.mean(jnp.square(xf - mu), axis=-1, keepdims=True)
    y = (xf - mu) * lax.rsqrt(var + EPS) * g.astype(jnp.float32) + b.astype(jnp.float32)
    return y.astype(x.dtype)


def rope_partial(x, pos):
    half = ROPE_DIM // 2
    inv_freq = jnp.exp(-math.log(ROPE_THETA) * jnp.arange(half, dtype=jnp.float32) / half)
    ang = pos.astype(jnp.float32)[:, None] * inv_freq[None, :]
    cos = jnp.cos(ang)[:, None, None, :]
    sin = jnp.sin(ang)[:, None, None, :]
    xf = x.astype(jnp.float32)
    x1 = xf[..., :half]
    x2 = xf[..., half:ROPE_DIM]
    out = jnp.concatenate([x1 * cos - x2 * sin, x2 * cos + x1 * sin, xf[..., ROPE_DIM:]], axis=-1)
    return out.astype(x.dtype)


def causal_dwconv(u, prev, w):
    ext = jnp.concatenate([prev.astype(u.dtype), u], axis=1)
    y = lax.conv_general_dilated(ext, w[:, None, :].astype(u.dtype), window_strides=(1,), padding='VALID',
                                 dimension_numbers=('NWC', 'WIO', 'NWC'), feature_group_count=u.shape[-1])
    return y, ext[:, ext.shape[1] - (w.shape[0] - 1):]


def diff_lambda(lq1, lk1, lq2, lk2, lam_init):
    f = lambda a: a.astype(jnp.float32)
    return jnp.exp(jnp.sum(f(lq1) * f(lk1))) - jnp.exp(jnp.sum(f(lq2) * f(lk2))) + lam_init


def diff_weights(s, lam):
    pr = jax.nn.softmax(s, axis=-1)
    return pr[..., 0, :, :] - lam * pr[..., 1, :, :]


def diff_attn_prompt(q, k, v, lam):
    b, s_len = q.shape[0], q.shape[1]
    nb = s_len // Q_BLOCK
    qb = jnp.moveaxis(q.reshape(b, nb, Q_BLOCK, N_HEADS, 2, HEAD_DIM), 1, 0)
    vf = v.astype(jnp.float32)
    kpos = jnp.arange(s_len, dtype=jnp.int32)
    scale = HEAD_DIM ** -0.5

    def block(args):
        qi, i = args
        s = jnp.einsum('bqhcd,bkhcd->bhcqk', qi, k, preferred_element_type=jnp.float32) * scale
        qpos = i * Q_BLOCK + jnp.arange(Q_BLOCK, dtype=jnp.int32)
        s = jnp.where(kpos[None, :] <= qpos[:, None], s, -jnp.inf)
        a = diff_weights(s, lam)
        return jnp.einsum('bhqk,bkhv->bqhv', a, vf)

    o = lax.map(block, (qb, jnp.arange(nb, dtype=jnp.int32)))
    return jnp.moveaxis(o, 0, 1).reshape(b, s_len, N_HEADS, V_DIM)


def diff_attn_sample(q, k, v, cache_k, cache_v, li, page_table, lam):
    scale = HEAD_DIM ** -0.5

    def one(args):
        qi, ki, vi, pt = args
        kp = cache_k[li, pt].reshape(-1, N_HEADS, 2, HEAD_DIM)
        vp = cache_v[li, pt].reshape(-1, N_HEADS, V_DIM)
        kk = jnp.concatenate([kp.astype(ki.dtype), ki], axis=0)
        vv = jnp.concatenate([vp.astype(vi.dtype), vi], axis=0).astype(jnp.float32)
        t_len, l_len = qi.shape[0], kk.shape[0]
        s = jnp.einsum('qhcd,khcd->hcqk', qi, kk, preferred_element_type=jnp.float32) * scale
        qpos = l_len - t_len + jnp.arange(t_len, dtype=jnp.int32)
        s = jnp.where(jnp.arange(l_len, dtype=jnp.int32)[None, :] <= qpos[:, None], s, -jnp.inf)
        a = diff_weights(s, lam)
        return jnp.einsum('hqk,khv->qhv', a, vv)

    return lax.map(one, (q, k, v, page_table))


def hybrid_mixer(h, li, layer_idx, pos, attend, sc_prev, p):
    b, s_len, _ = h.shape
    z = h @ p['hy_w_in'][li]
    cuts = [QK_WIDTH, 2 * QK_WIDTH, 2 * QK_WIDTH + ATT_WIDTH, 2 * QK_WIDTH + ATT_WIDTH + SC_WIDTH,
            2 * QK_WIDTH + ATT_WIDTH + 2 * SC_WIDTH]
    q, k, v, gate_b, gate_c, u_in = jnp.split(z, cuts, axis=-1)
    q = rope_partial(rmsnorm(q.reshape(b, s_len, N_HEADS, 2, HEAD_DIM), p['q_norm_g'][li]), pos)
    k = rope_partial(rmsnorm(k.reshape(b, s_len, N_HEADS, 2, HEAD_DIM), p['k_norm_g'][li]), pos)
    v = v.reshape(b, s_len, N_HEADS, V_DIM)
    lam_init = 0.8 - 0.6 * math.exp(-0.3 * layer_idx)
    lam = diff_lambda(p['lambda_q1'][li], p['lambda_k1'][li], p['lambda_q2'][li], p['lambda_k2'][li], lam_init)
    o = attend(q, k, v, lam, li)
    o = (rmsnorm(o, p['subln_g'][li]) * (1.0 - lam_init)).reshape(b, s_len, ATT_WIDTH).astype(h.dtype)
    conv, sc_new = causal_dwconv(gate_c * u_in, sc_prev, p['sconv_w'][li])
    o_sc = gate_b * conv
    out = jnp.concatenate([o, o_sc], axis=-1) @ p['hy_w_out'][li]
    return out, k.reshape(b, s_len, N_HEADS, 2 * HEAD_DIM), v, sc_new


def conformer_conv(h, ci, conf_prev, p):
    u = h @ p['conf_w1'][ci] + p['conf_b1'][ci]
    a, g = jnp.split(u, 2, axis=-1)
    u = a * jax.nn.sigmoid(g)
    y, st = causal_dwconv(u, conf_prev, p['conf_dw'][ci])
    y = jax.nn.silu(layernorm(y + p['conf_dw_b'][ci], p['conf_ln_g'][ci], p['conf_ln_b'][ci]))
    return y @ p['conf_w2'][ci] + p['conf_b2'][ci], st


def expert_ffn(xb, w_gu, b_gu, w_dn, b_dn):
    hg = xb @ w_gu + b_gu
    g, lin = jnp.split(hg, 2, axis=-1)
    g = jnp.minimum(g, SWIGLU_LIMIT)
    lin = jnp.clip(lin, -SWIGLU_LIMIT, SWIGLU_LIMIT)
    return (g * jax.nn.sigmoid(SWIGLU_ALPHA * g) * (lin + 1.0)) @ w_dn + b_dn


def moe_grouped(t, top_i, top_w, w_gu, b_gu, w_dn, b_dn):
    n_tok, d = t.shape
    n_asg = n_tok * TOP_K
    flat_e = top_i.reshape(n_asg)
    order = jnp.argsort(flat_e)
    sorted_e = flat_e[order]
    counts = jnp.zeros((N_EXPERTS,), jnp.int32).at[flat_e].add(1)
    padded = (counts + MOE_BLOCK - 1) // MOE_BLOCK * MOE_BLOCK
    pad_end = jnp.cumsum(padded)
    pad_start = pad_end - padded
    start = jnp.cumsum(counts) - counts
    dest_sorted = pad_start[sorted_e] + jnp.arange(n_asg, dtype=jnp.int32) - start[sorted_e]
    dest = jnp.zeros((n_asg,), jnp.int32).at[order].set(dest_sorted)
    n_blocks = -(-n_asg // MOE_BLOCK) + N_EXPERTS
    row_tok = jnp.zeros((n_blocks * MOE_BLOCK,), jnp.int32).at[dest].set(
        jnp.arange(n_asg, dtype=jnp.int32) // TOP_K)
    block_e = jnp.minimum(jnp.searchsorted(pad_end, jnp.arange(n_blocks, dtype=jnp.int32) * MOE_BLOCK,
                                           side='right'), N_EXPERTS - 1)
    xb = t[row_tok].reshape(n_blocks, MOE_BLOCK, d)

    def run_block(args):
        xi, e = args
        return expert_ffn(xi, w_gu[e], b_gu[e], w_dn[e], b_dn[e])

    yb = lax.map(run_block, (xb, block_e)).reshape(n_blocks * MOE_BLOCK, d)
    y = yb[dest].reshape(n_tok, TOP_K, d)
    return jnp.einsum('tkd,tk->td', y, top_w)


def moe_ffn(h, l, p):
    b, s_len, d = h.shape
    t = h.reshape(b * s_len, d)
    logits = jnp.dot(t, p['router_w'][l], preferred_element_type=jnp.float32) + p['router_b'][l].astype(jnp.float32)
    top_v, top_i = lax.top_k(logits, TOP_K)
    top_w = jax.nn.softmax(top_v, axis=-1).astype(h.dtype)
    y = moe_grouped(t, top_i, top_w, p['moe_w_gu'][l], p['moe_b_gu'][l], p['moe_w_dn'][l], p['moe_b_dn'][l])
    return y.reshape(b, s_len, d)


def trunk(x, c, pos, attend, sc_states, conf_states, p):
    k_rows, v_rows, sc_out, conf_out = [], [], [], []
    for l in range(DEPTH):
        mod = jax.nn.silu(c) @ p['ada_w'][l] + p['ada_b'][l]
        sh1, sc1, g1, sh2, sc2, g2 = [m[:, None, :] for m in jnp.split(mod, 6, axis=-1)]
        h = rmsnorm(x, p['norm_mix_g'][l]) * (1.0 + sc1) + sh1
        if l % 2 == 0:
            li = l // 2
            mix, kr, vr, scn = hybrid_mixer(h, li, l, pos, attend, sc_states[li], p)
            k_rows.append(kr)
            v_rows.append(vr)
            sc_out.append(scn)
        else:
            ci = l // 2
            mix, cfn = conformer_conv(h, ci, conf_states[ci], p)
            conf_out.append(cfn)
        x = x + g1 * mix
        h = rmsnorm(x, p['norm_ffn_g'][l]) * (1.0 + sc2) + sh2
        x = x + g2 * moe_ffn(h, l, p)
    return x, jnp.stack(k_rows), jnp.stack(v_rows), jnp.stack(sc_out), jnp.stack(conf_out)


def setup_inputs(seed: int = 0) -> dict:
    key = jax.random.key(seed)
    ks = jax.random.split(key, 40)
    f32 = jnp.float32

    def nrm(k, shape, scale):
        return jax.random.normal(k, shape, f32) * scale

    def gain(k, shape):
        return 1.0 + nrm(k, shape, 0.02)

    n_pages = PAST_LEN // PAGE_SIZE
    n_used = DEC_BATCH * n_pages
    n_pool = n_used + n_used // 4
    page_table = jax.random.permutation(ks[6], n_pool)[:n_used].reshape(DEC_BATCH, n_pages).astype(jnp.int32)
    d = D_MODEL
    return {
        'x_prompt': nrm(ks[0], (BATCH, SEQ, d), 1.0),
        'x_sample': nrm(ks[1], (DEC_BATCH, DEC_SEQ, d), 1.0),
        'cache_k': nrm(ks[2], (N_EVEN, n_pool, PAGE_SIZE, N_HEADS, 2 * HEAD_DIM), 1.0),
        'cache_v': nrm(ks[3], (N_EVEN, n_pool, PAGE_SIZE, N_HEADS, V_DIM), 1.0),
        'state_sconv': nrm(ks[4], (N_EVEN, DEC_BATCH, SC_KERNEL - 1, SC_WIDTH), 1.0),
        'state_conf': nrm(ks[5], (N_ODD, DEC_BATCH, CONF_KERNEL - 1, CONF_DIM), 0.5),
        'page_table': page_table,
        'c_prompt': nrm(ks[7], (BATCH, d), 1.0),
        'c_sample': nrm(ks[8], (DEC_BATCH, d), 1.0),
        'ada_w': nrm(ks[9], (DEPTH, d, 6 * d), 0.3 * d ** -0.5),
        'ada_b': nrm(ks[10], (DEPTH, 6 * d), 0.02),
        'norm_mix_g': gain(ks[11], (DEPTH, d)),
        'norm_ffn_g': gain(ks[12], (DEPTH, d)),
        'hy_w_in': nrm(ks[13], (N_EVEN, d, IN_WIDTH), d ** -0.5),
        'hy_w_out': nrm(ks[14], (N_EVEN, OUT_WIDTH, d), OUT_WIDTH ** -0.5),
        'q_norm_g': gain(ks[15], (N_EVEN, HEAD_DIM)),
        'k_norm_g': gain(ks[16], (N_EVEN, HEAD_DIM)),
        'lambda_q1': nrm(ks[17], (N_EVEN, HEAD_DIM), 0.1),
        'lambda_k1': nrm(ks[18], (N_EVEN, HEAD_DIM), 0.1),
        'lambda_q2': nrm(ks[19], (N_EVEN, HEAD_DIM), 0.1),
        'lambda_k2': nrm(ks[20], (N_EVEN, HEAD_DIM), 0.1),
        'subln_g': gain(ks[21], (N_EVEN, V_DIM)),
        'sconv_w': nrm(ks[22], (N_EVEN, SC_KERNEL, SC_WIDTH), SC_KERNEL ** -0.5),
        'conf_w1': nrm(ks[23], (N_ODD, d, 2 * CONF_DIM), d ** -0.5),
        'conf_b1': nrm(ks[24], (N_ODD, 2 * CONF_DIM), 0.02),
        'conf_dw': nrm(ks[25], (N_ODD, CONF_KERNEL, CONF_DIM), CONF_KERNEL ** -0.5),
        'conf_dw_b': nrm(ks[26], (N_ODD, CONF_DIM), 0.02),
        'conf_ln_g': gain(ks[27], (N_ODD, CONF_DIM)),
        'conf_ln_b': nrm(ks[28], (N_ODD, CONF_DIM), 0.02),
        'conf_w2': nrm(ks[29], (N_ODD, CONF_DIM, d), CONF_DIM ** -0.5),
        'conf_b2': nrm(ks[30], (N_ODD, d), 0.02),
        'router_w': nrm(ks[31], (DEPTH, d, N_EXPERTS), d ** -0.5),
        'router_b': nrm(ks[32], (DEPTH, N_EXPERTS), 0.01),
        'moe_w_gu': nrm(ks[33], (DEPTH, N_EXPERTS, d, 2 * D_FF), d ** -0.5),
        'moe_b_gu': nrm(ks[34], (DEPTH, N_EXPERTS, 2 * D_FF), 0.02),
        'moe_w_dn': nrm(ks[35], (DEPTH, N_EXPERTS, D_FF, d), D_FF ** -0.5),
        'moe_b_dn': nrm(ks[36], (DEPTH, N_EXPERTS, d), 0.02),
    }


def reference(x_prompt, x_sample, cache_k, cache_v, state_sconv, state_conf, page_table, c_prompt, c_sample,
              ada_w, ada_b, norm_mix_g, norm_ffn_g, hy_w_in, hy_w_out, q_norm_g, k_norm_g,
              lambda_q1, lambda_k1, lambda_q2, lambda_k2, subln_g, sconv_w,
              conf_w1, conf_b1, conf_dw, conf_dw_b, conf_ln_g, conf_ln_b, conf_w2, conf_b2,
              router_w, router_b, moe_w_gu, moe_b_gu, moe_w_dn, moe_b_dn):
    p = {'ada_w': ada_w, 'ada_b': ada_b, 'norm_mix_g': norm_mix_g, 'norm_ffn_g': norm_ffn_g,
         'hy_w_in': hy_w_in, 'hy_w_out': hy_w_out, 'q_norm_g': q_norm_g, 'k_norm_g': k_norm_g,
         'lambda_q1': lambda_q1, 'lambda_k1': lambda_k1, 'lambda_q2': lambda_q2, 'lambda_k2': lambda_k2,
         'subln_g': subln_g, 'sconv_w': sconv_w,
         'conf_w1': conf_w1, 'conf_b1': conf_b1, 'conf_dw': conf_dw, 'conf_dw_b': conf_dw_b,
         'conf_ln_g': conf_ln_g, 'conf_ln_b': conf_ln_b, 'conf_w2': conf_w2, 'conf_b2': conf_b2,
         'router_w': router_w, 'router_b': router_b, 'moe_w_gu': moe_w_gu, 'moe_b_gu': moe_b_gu,
         'moe_w_dn': moe_w_dn, 'moe_b_dn': moe_b_dn}
    past_len = page_table.shape[1] * PAGE_SIZE
    pos_prompt = jnp.arange(x_prompt.shape[1], dtype=jnp.int32)
    pos_sample = past_len + jnp.arange(x_sample.shape[1], dtype=jnp.int32)

    def attend_prompt(q, k, v, lam, li):
        return diff_attn_prompt(q, k, v, lam)

    def attend_sample(q, k, v, lam, li):
        return diff_attn_sample(q, k, v, cache_k, cache_v, li, page_table, lam)

    bp = x_prompt.shape[0]
    sc_zero = jnp.zeros((N_EVEN, bp, SC_KERNEL - 1, SC_WIDTH), x_prompt.dtype)
    conf_zero = jnp.zeros((N_ODD, bp, CONF_KERNEL - 1, CONF_DIM), x_prompt.dtype)
    y_prompt, k_prompt, v_prompt, sconv_prompt, conf_prompt = trunk(
        x_prompt, c_prompt, pos_prompt, attend_prompt, sc_zero, conf_zero, p)
    y_sample, k_sample, v_sample, sconv_sample, conf_sample = trunk(
        x_sample, c_sample, pos_sample, attend_sample, state_sconv, state_conf, p)
    return (y_prompt, y_sample, k_prompt, v_prompt, sconv_prompt, conf_prompt,
            k_sample, v_sample, sconv_sample, conf_sample)
```

```python
import functools
import math

import jax
import jax.numpy as jnp
from jax import lax
from jax.experimental import pallas as pl
from jax.experimental.pallas import tpu as pltpu

F32 = jnp.float32
BF16 = jnp.bfloat16
I32 = jnp.int32

EPS = 1e-6
HEAD_DIM = 64
V_DIM = 2 * HEAD_DIM
ROPE_DIM = HEAD_DIM // 4
ROPE_THETA = 500000.0
SC_KERNEL = 3
CONF_KERNEL = 31
N_EXPERTS = 32
TOP_K = 4
SWIGLU_LIMIT = 7.0
SWIGLU_ALPHA = 1.702
PAGE_SIZE = 128

VMEM_LIMIT_BYTES = 56 * 1024 * 1024
SUBLANES = 8
MOE_ROW_ALIGN = 8


def _params(*sem):
    return pltpu.CompilerParams(dimension_semantics=sem, vmem_limit_bytes=VMEM_LIMIT_BYTES)


def _sigmoid(x):
    return 1.0 / (1.0 + jnp.exp(-x))


def _dot(a, b):
    return jnp.dot(a, b, preferred_element_type=F32)


def _dot_nt(a, b):
    return lax.dot_general(a, b, (((1,), (1,)), ((), ())), preferred_element_type=F32)


def _rmsnorm_mod(x, g, sc, sh):
    ms = jnp.mean(x * x, axis=-1, keepdims=True)
    return x * lax.rsqrt(ms + EPS) * g * (1.0 + sc) + sh


def _ada_kernel(c_ref, w_ref, b_ref, o_ref):
    c = c_ref[...]
    s = c * _sigmoid(c)
    o_ref[0] = jnp.dot(s, w_ref[0], precision=lax.Precision.HIGHEST, preferred_element_type=F32) + b_ref[0]


def _ada_mod(c_all, ada_w, ada_b):
    depth, d, n = ada_w.shape
    r = c_all.shape[0]
    tn = 1536
    return pl.pallas_call(
        _ada_kernel,
        out_shape=jax.ShapeDtypeStruct((depth, r, n), F32),
        grid=(depth, n // tn),
        in_specs=[pl.BlockSpec((r, d), lambda l, j: (0, 0)),
                  pl.BlockSpec((1, d, tn), lambda l, j: (l, 0, j)),
                  pl.BlockSpec((1, 1, tn), lambda l, j: (l, 0, j))],
        out_specs=pl.BlockSpec((1, r, tn), lambda l, j: (l, 0, j)),
        compiler_params=_params("arbitrary", "arbitrary"),
        name="ada_mod",
    )(c_all, ada_w, ada_b.reshape(depth, 1, n))


def _nmm_kernel(*refs, glu, has_bias):
    x_ref, g_ref, sc_ref, sh_ref = refs[:4]
    rest = list(refs[4:])
    w_refs = [rest.pop(0) for _ in range(2 if glu else 1)]
    b_refs = [rest.pop(0) for _ in range((2 if glu else 1) if has_bias else 0)]
    o_ref, h_scr = rest

    @pl.when(pl.program_id(1) == 0)
    def _():
        h = _rmsnorm_mod(x_ref[...], g_ref[...], sc_ref[0], sh_ref[0])
        h_scr[...] = h.astype(BF16)

    h = h_scr[...]
    accs = []
    for i, w_ref in enumerate(w_refs):
        a = _dot(h, w_ref[...])
        if has_bias:
            a = a + b_refs[i][...]
        accs.append(a)
    if glu:
        o_ref[...] = (accs[0] * _sigmoid(accs[1])).astype(o_ref.dtype)
    else:
        o_ref[...] = accs[0].astype(o_ref.dtype)


def _norm_mod_matmul(x, g, sc, sh, w, bias, *, tm, tn, glu=False, out_dtype=F32):
    t, d = x.shape
    n_out = w.shape[1] // 2 if glu else w.shape[1]
    ns, r, _ = sc.shape
    tps = (t // tm) // ns
    nj = n_out // tn
    in_specs = [pl.BlockSpec((tm, d), lambda i, j: (i, 0)),
                pl.BlockSpec((1, d), lambda i, j: (0, 0)),
                pl.BlockSpec((1, r, d), lambda i, j: (i // tps, 0, 0)),
                pl.BlockSpec((1, r, d), lambda i, j: (i // tps, 0, 0)),
                pl.BlockSpec((d, tn), lambda i, j: (0, j))]
    args = [x, g.reshape(1, d), sc, sh, w]
    if glu:
        in_specs.append(pl.BlockSpec((d, tn), lambda i, j: (0, j + nj)))
        args.append(w)
    if bias is not None:
        b2 = bias.reshape(1, -1)
        in_specs.append(pl.BlockSpec((1, tn), lambda i, j: (0, j)))
        args.append(b2)
        if glu:
            in_specs.append(pl.BlockSpec((1, tn), lambda i, j: (0, j + nj)))
            args.append(b2)
    return pl.pallas_call(
        functools.partial(_nmm_kernel, glu=glu, has_bias=bias is not None),
        out_shape=jax.ShapeDtypeStruct((t, n_out), out_dtype),
        grid=(t // tm, nj),
        in_specs=in_specs,
        out_specs=pl.BlockSpec((tm, tn), lambda i, j: (i, j)),
        scratch_shapes=[pltpu.VMEM((tm, d), BF16)],
        compiler_params=_params("arbitrary", "arbitrary"),
        name="norm_mod_matmul",
    )(*args)


def _group_mean_square(xh):
    r = lax.broadcasted_iota(I32, (V_DIM, V_DIM), 0) // HEAD_DIM
    c = lax.broadcasted_iota(I32, (V_DIM, V_DIM), 1) // HEAD_DIM
    ones_blk = jnp.where(r == c, 1.0, 0.0).astype(BF16)
    x2 = xh * xh
    hi = x2.astype(BF16)
    lo = (x2 - hi.astype(F32)).astype(BF16)
    return (_dot(hi, ones_blk) + _dot(lo, ones_blk)) * (1.0 / HEAD_DIM)


def _qk_post_kernel(q_ref, k_ref, v_ref, cos_ref, s1_ref, s2_ref, qg_ref, kg_ref,
                    krow_ref, vrow_ref, qb_ref, kb_ref, vb_ref, *, n_heads):
    cos, s1, s2 = cos_ref[...], s1_ref[...], s2_ref[...]

    def norm_rope(xh, g):
        y = xh * lax.rsqrt(_group_mean_square(xh) + EPS) * g
        return y * cos + pltpu.roll(y, V_DIM - ROPE_DIM // 2, 1) * s1 + pltpu.roll(y, ROPE_DIM // 2, 1) * s2

    for h in range(n_heads):
        sl = slice(h * V_DIM, (h + 1) * V_DIM)
        qh = norm_rope(q_ref[:, sl], qg_ref[...])
        kh = norm_rope(k_ref[:, sl], kg_ref[...])
        krow_ref[:, sl] = kh
        kb_ref[:, sl] = kh.astype(BF16)
        qb_ref[:, sl] = (qh * (HEAD_DIM ** -0.5)).astype(BF16)
    v = v_ref[...]
    vrow_ref[...] = v
    vb_ref[...] = v.astype(BF16)


def _rope_tables(pos):
    half = ROPE_DIM // 2
    inv_freq = jnp.exp(-math.log(ROPE_THETA) * jnp.arange(half, dtype=F32) / half)
    ang = pos.astype(F32)[:, None] * inv_freq[None, :]
    cos, sin = jnp.cos(ang), jnp.sin(ang)
    p = pos.shape[0]
    ones = jnp.ones((p, HEAD_DIM - ROPE_DIM), F32)
    zeros = jnp.zeros((p, HEAD_DIM - ROPE_DIM), F32)
    z8 = jnp.zeros((p, half), F32)
    c64 = jnp.concatenate([cos, cos, ones], axis=1)
    s1_64 = jnp.concatenate([-sin, z8, zeros], axis=1)
    s2_64 = jnp.concatenate([z8, sin, zeros], axis=1)
    two = lambda a: jnp.concatenate([a, a], axis=1)
    return two(c64), two(s1_64), two(s2_64)


def _qk_post(z, tables, q_g, k_g, *, tm, d):
    t = z.shape[0]
    n_heads = d // V_DIM
    p = tables[0].shape[0]
    npb = p // tm
    row = lambda c: pl.BlockSpec((tm, d), lambda i: (i, c))
    tab = pl.BlockSpec((tm, V_DIM), lambda i: (i % npb, 0))
    gspec = pl.BlockSpec((1, V_DIM), lambda i: (0, 0))
    out = pl.BlockSpec((tm, d), lambda i: (i, 0))
    g2 = lambda g: jnp.concatenate([g, g]).reshape(1, V_DIM)
    return pl.pallas_call(
        functools.partial(_qk_post_kernel, n_heads=n_heads),
        out_shape=[jax.ShapeDtypeStruct((t, d), F32), jax.ShapeDtypeStruct((t, d), F32),
                   jax.ShapeDtypeStruct((t, d), BF16), jax.ShapeDtypeStruct((t, d), BF16),
                   jax.ShapeDtypeStruct((t, d), BF16)],
        grid=(t // tm,),
        in_specs=[row(0), row(1), row(2), tab, tab, tab, gspec, gspec],
        out_specs=[out, out, out, out, out],
        compiler_params=_params("arbitrary"),
        name="qk_post",
    )(z, z, z, *tables, g2(q_g), g2(k_g))


def _diff_lambda(lq1, lk1, lq2, lk2, lam_init):
    a = jnp.sum(lq1 * lk1, axis=-1, keepdims=True)
    b = jnp.sum(lq2 * lk2, axis=-1, keepdims=True)
    return jnp.exp(a) - jnp.exp(b) + lam_init


def _subln(o, g, lam_init):
    ms = jnp.mean(o * o, axis=-1, keepdims=True)
    return o * lax.rsqrt(ms + EPS) * g * (1.0 - lam_init)


def _flash_kernel(q_ref, k_ref, v_ref, lq1, lk1, lq2, lk2, sg_ref, o_ref,
                  qm_scr, m_scr, l_scr, acc_scr, *, lam_init, tq):
    qi, ki, nk = pl.program_id(2), pl.program_id(3), pl.num_programs(3)

    @pl.when(ki == 0)
    def _():
        q = q_ref[...]
        lane = lax.broadcasted_iota(I32, q.shape, 1)
        zero = jnp.zeros_like(q)
        qm_scr[0] = jnp.where(lane < HEAD_DIM, q, zero)
        qm_scr[1] = jnp.where(lane >= HEAD_DIM, q, zero)
        m_scr[...] = jnp.full(m_scr.shape, -jnp.inf, F32)
        l_scr[...] = jnp.zeros(l_scr.shape, F32)
        acc_scr[...] = jnp.zeros(acc_scr.shape, F32)

    @pl.when(ki <= qi)
    def _():
        k = k_ref[...]
        v = v_ref[...]
        row = lax.broadcasted_iota(I32, (tq, tq), 0)
        col = lax.broadcasted_iota(I32, (tq, tq), 1)
        visible = (col <= row) | (ki < qi)
        for c in range(2):
            s = _dot_nt(qm_scr[c], k)
            s = jnp.where(visible, s, -jnp.inf)
            m_old = m_scr[c]
            m_new = jnp.maximum(m_old, jnp.max(s, axis=-1, keepdims=True))
            alpha = jnp.exp(m_old - m_new)
            p = jnp.exp(s - m_new)
            l_scr[c] = alpha * l_scr[c] + jnp.sum(p, axis=-1, keepdims=True)
            acc_scr[c] = alpha * acc_scr[c] + _dot(p.astype(BF16), v)
            m_scr[c] = m_new

    @pl.when(ki == nk - 1)
    def _():
        lam = _diff_lambda(lq1[...], lk1[...], lq2[...], lk2[...], lam_init)
        o = acc_scr[0] / l_scr[0] - lam * (acc_scr[1] / l_scr[1])
        o_ref[...] = _subln(o, sg_ref[...], lam_init).astype(o_ref.dtype)


def _flash_prompt(qb, kb, vb, lams, subln_g, *, b, s_len, d, lam_init, tq):
    n_heads = d // V_DIM
    nq = s_len // tq
    lam_spec = pl.BlockSpec((1, HEAD_DIM), lambda bi, h, qi, ki: (0, 0))
    return pl.pallas_call(
        functools.partial(_flash_kernel, lam_init=lam_init, tq=tq),
        out_shape=jax.ShapeDtypeStruct((b * s_len, d), BF16),
        grid=(b, n_heads, nq, nq),
        in_specs=[pl.BlockSpec((tq, V_DIM), lambda bi, h, qi, ki: (bi * nq + qi, h)),
                  pl.BlockSpec((tq, V_DIM), lambda bi, h, qi, ki: (bi * nq + jnp.minimum(ki, qi), h)),
                  pl.BlockSpec((tq, V_DIM), lambda bi, h, qi, ki: (bi * nq + jnp.minimum(ki, qi), h)),
                  lam_spec, lam_spec, lam_spec, lam_spec,
                  pl.BlockSpec((1, V_DIM), lambda bi, h, qi, ki: (0, 0))],
        out_specs=pl.BlockSpec((tq, V_DIM), lambda bi, h, qi, ki: (bi * nq + qi, h)),
        scratch_shapes=[pltpu.VMEM((2, tq, V_DIM), BF16), pltpu.VMEM((2, tq, 1), F32),
                        pltpu.VMEM((2, tq, 1), F32), pltpu.VMEM((2, tq, V_DIM), F32)],
        compiler_params=_params("arbitrary", "arbitrary", "arbitrary", "arbitrary"),
        name="flash_prompt",
    )(qb, kb, vb, *[x.reshape(1, HEAD_DIM) for x in lams], subln_g.reshape(1, V_DIM))


def _decode_kernel(pt_ref, q_ref, kn_ref, vn_ref, lq1, lk1, lq2, lk2, sg_ref, *refs,
                   lam_init, n_heads, pages, t_new):
    k_refs, v_refs = refs[:pages], refs[pages:2 * pages]
    o_ref, qbd_scr, m_scr, l_scr, acc_scr = refs[2 * pages:]
    ci, nc = pl.program_id(1), pl.num_programs(1)
    rows = 2 * t_new * n_heads
    d = n_heads * V_DIM

    def online_update(s, v_all):
        m_old = m_scr[...]
        m_new = jnp.maximum(m_old, jnp.max(s, axis=-1, keepdims=True))
        alpha = jnp.exp(m_old - m_new)
        p = jnp.exp(s - m_new)
        l_scr[...] = alpha * l_scr[...] + jnp.sum(p, axis=-1, keepdims=True)
        acc_scr[...] = alpha * acc_scr[...] + _dot(p.astype(BF16), v_all)
        m_scr[...] = m_new

    @pl.when(ci == 0)
    def _():
        q = q_ref[0]
        q8 = jnp.concatenate([q, q], axis=0)
        qt = jnp.broadcast_to(q8[None], (n_heads, 2 * t_new, d)).reshape(rows, d)
        r = lax.broadcasted_iota(I32, (rows, d), 0)
        lane = lax.broadcasted_iota(I32, (rows, d), 1)
        qbd_scr[...] = jnp.where(lane // HEAD_DIM == r // t_new, qt, 0.0).astype(BF16)
        kn = kn_ref[0]
        vn = vn_ref[0]
        pad = jnp.zeros((SUBLANES - t_new, d), F32)
        kn8 = jnp.concatenate([kn, pad], axis=0).astype(BF16)
        vn8 = jnp.concatenate([vn, pad], axis=0).astype(BF16)
        s = _dot_nt(qbd_scr[...], kn8)
        rr = lax.broadcasted_iota(I32, s.shape, 0) % t_new
        cc = lax.broadcasted_iota(I32, s.shape, 1)
        s = jnp.where(cc <= rr, s, -jnp.inf)
        m = jnp.max(s, axis=-1, keepdims=True)
        p = jnp.exp(s - m)
        m_scr[...] = m
        l_scr[...] = jnp.sum(p, axis=-1, keepdims=True)
        acc_scr[...] = _dot(p.astype(BF16), vn8)

    def heads_to_lanes(ref):
        return jnp.concatenate(
            [ref[0, pl.ds(h, PAGE_SIZE, stride=n_heads), :].astype(BF16) for h in range(n_heads)], axis=1)

    k_all = jnp.concatenate([heads_to_lanes(r) for r in k_refs], axis=0)
    v_all = jnp.concatenate([heads_to_lanes(r) for r in v_refs], axis=0)
    online_update(_dot_nt(qbd_scr[...], k_all), v_all)

    @pl.when(ci == nc - 1)
    def _():
        lam = _diff_lambda(lq1[...], lk1[...], lq2[...], lk2[...], lam_init)
        o = acc_scr[...] / l_scr[...]
        outs = []
        for h in range(n_heads):
            blk = o[h * 2 * t_new:(h + 1) * 2 * t_new, h * V_DIM:(h + 1) * V_DIM]
            oh = blk[:t_new] - lam * blk[t_new:]
            outs.append(_subln(oh, sg_ref[...], lam_init))
        o_ref[0] = jnp.concatenate(outs, axis=1).astype(o_ref.dtype)


def _decode_attn(q, k_new, v_new, cache_k, cache_v, li, page_table, lams, subln_g, *, lam_init, pages):
    bd, t_new, d = q.shape
    n_heads = d // V_DIM
    n_pool = cache_k.shape[1]
    rows_per_page = PAGE_SIZE * n_heads
    ck = cache_k.reshape(-1, rows_per_page, V_DIM)
    cv = cache_v.reshape(-1, rows_per_page, V_DIM)
    n_pages = page_table.shape[1]
    nc = n_pages // pages
    rows = 2 * t_new * n_heads

    def page_spec(p):
        return pl.BlockSpec((1, rows_per_page, V_DIM),
                            lambda b, c, pt: (li * n_pool + pt[b, c * pages + p], 0, 0))

    seq = pl.BlockSpec((1, t_new, d), lambda b, c, pt: (b, 0, 0))
    lam_spec = pl.BlockSpec((1, HEAD_DIM), lambda b, c, pt: (0, 0))
    grid_spec = pltpu.PrefetchScalarGridSpec(
        num_scalar_prefetch=1,
        grid=(bd, nc),
        in_specs=[seq, seq, seq, lam_spec, lam_spec, lam_spec, lam_spec,
                  pl.BlockSpec((1, V_DIM), lambda b, c, pt: (0, 0))]
                 + [page_spec(p) for p in range(pages)] + [page_spec(p) for p in range(pages)],
        out_specs=pl.BlockSpec((1, t_new, d), lambda b, c, pt: (b, 0, 0)),
        scratch_shapes=[pltpu.VMEM((rows, d), BF16), pltpu.VMEM((rows, 1), F32),
                        pltpu.VMEM((rows, 1), F32), pltpu.VMEM((rows, d), F32)])
    return pl.pallas_call(
        functools.partial(_decode_kernel, lam_init=lam_init, n_heads=n_heads, pages=pages, t_new=t_new),
        out_shape=jax.ShapeDtypeStruct((bd, t_new, d), BF16),
        grid_spec=grid_spec,
        compiler_params=_params("arbitrary", "arbitrary"),
        name="decode_attn",
    )(page_table, q, k_new, v_new, *[x.reshape(1, HEAD_DIM) for x in lams], subln_g.reshape(1, V_DIM),
      *([ck] * pages), *([cv] * pages))


def _layernorm_silu(y, g, b):
    mu = jnp.mean(y, axis=-1, keepdims=True)
    yc = y - mu
    var = jnp.mean(yc * yc, axis=-1, keepdims=True)
    z = yc * lax.rsqrt(var + EPS) * g + b
    return z * _sigmoid(z)


def _conv_prompt_kernel(*refs, taps, halo, tt, gated, ln):
    refs = list(refs)
    u_ref, uh_ref, prev_ref = refs[:3]
    del refs[:3]
    if gated:
        gc_ref, gch_ref, gb_ref = refs[:3]
        del refs[:3]
    w_ref = refs.pop(0)
    if ln:
        cb_ref, lg_ref, lb_ref = refs[:3]
        del refs[:3]
    o_ref = refs.pop(0)
    if gated:
        tail_ref = refs.pop(0)
    ext = refs.pop(0)
    ti, nt = pl.program_id(1), pl.num_programs(1)

    cur = u_ref[...]
    if gated:
        cur = cur * gc_ref[...]
    ext[pl.ds(halo, tt), :] = cur

    @pl.when(ti == 0)
    def _():
        ext[pl.ds(0, halo), :] = prev_ref[0]

    @pl.when(ti > 0)
    def _():
        hv = uh_ref[...]
        if gated:
            hv = hv * gch_ref[...]
        ext[pl.ds(0, halo), :] = hv

    base = halo - (taps - 1)
    y = w_ref[pl.ds(0, 1), :] * ext[pl.ds(base, tt), :]
    for j in range(1, taps):
        y = y + w_ref[pl.ds(j, 1), :] * ext[pl.ds(base + j, tt), :]
    if ln:
        y = _layernorm_silu(y + cb_ref[...], lg_ref[...], lb_ref[...])
    if gated:
        y = gb_ref[...] * y

        @pl.when(ti == nt - 1)
        def _():
            tail_ref[0] = ext[pl.ds(halo + tt - SUBLANES, SUBLANES), :]
    o_ref[...] = y.astype(o_ref.dtype)


def _conv_prompt(src, ucol, prev, w, *, b, s_len, c, tt, gate_cols=None, ln_params=None):
    taps = w.shape[0]
    halo = SUBLANES if taps - 1 <= SUBLANES else 32
    nt = s_len // tt
    hpt = tt // halo
    prev_p = jnp.pad(prev, ((0, 0), (halo - (taps - 1), 0), (0, 0)))
    cur = lambda col: pl.BlockSpec((tt, c), lambda bi, ti: (bi * nt + ti, col))
    hal = lambda col: pl.BlockSpec((halo, c), lambda bi, ti: (jnp.maximum((bi * nt + ti) * hpt - 1, 0), col))
    vec = pl.BlockSpec((1, c), lambda bi, ti: (0, 0))
    in_specs = [cur(ucol), hal(ucol), pl.BlockSpec((1, halo, c), lambda bi, ti: (bi, 0, 0))]
    args = [src, src, prev_p]
    gated = gate_cols is not None
    if gated:
        cc, cb = gate_cols
        in_specs += [cur(cc), hal(cc), cur(cb)]
        args += [src, src, src]
    in_specs.append(pl.BlockSpec((taps, c), lambda bi, ti: (0, 0)))
    args.append(w)
    if ln_params is not None:
        in_specs += [vec, vec, vec]
        args += [p.reshape(1, c) for p in ln_params]
    out_shape = [jax.ShapeDtypeStruct((b * s_len, c), BF16)]
    out_specs = [pl.BlockSpec((tt, c), lambda bi, ti: (bi * nt + ti, 0))]
    if gated:
        out_shape.append(jax.ShapeDtypeStruct((b, SUBLANES, c), F32))
        out_specs.append(pl.BlockSpec((1, SUBLANES, c), lambda bi, ti: (bi, 0, 0)))
    res = pl.pallas_call(
        functools.partial(_conv_prompt_kernel, taps=taps, halo=halo, tt=tt, gated=gated, ln=ln_params is not None),
        out_shape=out_shape,
        grid=(b, nt),
        in_specs=in_specs,
        out_specs=out_specs,
        scratch_shapes=[pltpu.VMEM((halo + tt, c), F32)],
        compiler_params=_params("arbitrary", "arbitrary"),
        name="conv_prompt",
    )(*args)
    return res if gated else res[0]


def _conv_sample_kernel(*refs, taps, s_new, gated, ln):
    refs = list(refs)
    prev_ref, u_ref = refs[:2]
    del refs[:2]
    if gated:
        gc_ref, gb_ref = refs[:2]
        del refs[:2]
    w_ref = refs.pop(0)
    if ln:
        cb_ref, lg_ref, lb_ref = refs[:3]
        del refs[:3]
    o_ref, cu_ref, ext = refs
    for j in range(taps - 1):
        ext[j] = prev_ref[j]
    for s in range(s_new):
        cu = u_ref[s]
        if gated:
            cu = cu * gc_ref[s]
        cu_ref[s] = cu
        ext[taps - 1 + s] = cu
    for s in range(s_new):
        y = w_ref[0] * ext[s]
        for j in range(1, taps):
            y = y + w_ref[j] * ext[s + j]
        if ln:
            y = _layernorm_silu(y + cb_ref[...], lg_ref[...], lb_ref[...])
        if gated:
            y = gb_ref[s] * y
        o_ref[s] = y.astype(o_ref.dtype)


def _conv_sample(u, prev, w, *, gates=None, ln_params=None):
    s_new, bd, c = u.shape
    taps = w.shape[0]
    args = [prev, u]
    gated = gates is not None
    if gated:
        args += list(gates)
    args.append(w.reshape(taps, 1, c))
    if ln_params is not None:
        args += [p.reshape(1, c) for p in ln_params]
    return pl.pallas_call(
        functools.partial(_conv_sample_kernel, taps=taps, s_new=s_new, gated=gated, ln=ln_params is not None),
        out_shape=[jax.ShapeDtypeStruct((s_new, bd, c), BF16), jax.ShapeDtypeStruct((s_new, bd, c), F32)],
        scratch_shapes=[pltpu.VMEM((taps - 1 + s_new, bd, c), F32)],
        compiler_params=pltpu.CompilerParams(vmem_limit_bytes=VMEM_LIMIT_BYTES),
        name="conv_sample",
    )(*args)


def _mm_res_kernel(*refs, n_in, has_bias):
    a_refs, w_refs = refs[:n_in], refs[n_in:2 * n_in]
    rest = list(refs[2 * n_in:])
    b_ref = rest.pop(0) if has_bias else None
    x_ref, gate_ref, o_ref = rest
    acc = _dot(a_refs[0][...], w_refs[0][...])
    for a_ref, w_ref in zip(a_refs[1:], w_refs[1:]):
        acc = acc + _dot(a_ref[...], w_ref[...])
    if has_bias:
        acc = acc + b_ref[...]
    o_ref[...] = x_ref[...] + gate_ref[0] * acc


def _matmul_residual(a_list, w_list, bias, x, gate, *, tm, tn):
    t, d = x.shape
    ns, r, _ = gate.shape
    tps = (t // tm) // ns
    n_in = len(a_list)
    in_specs = [pl.BlockSpec((tm, a.shape[1]), lambda i, j: (i, 0)) for a in a_list]
    in_specs += [pl.BlockSpec((w.shape[0], tn), lambda i, j: (0, j)) for w in w_list]
    args = list(a_list) + list(w_list)
    if bias is not None:
        in_specs.append(pl.BlockSpec((1, tn), lambda i, j: (0, j)))
        args.append(bias.reshape(1, d))
    in_specs += [pl.BlockSpec((tm, tn), lambda i, j: (i, j)),
                 pl.BlockSpec((1, r, tn), lambda i, j: (i // tps, 0, j))]
    args += [x, gate]
    return pl.pallas_call(
        functools.partial(_mm_res_kernel, n_in=n_in, has_bias=bias is not None),
        out_shape=jax.ShapeDtypeStruct((t, d), F32),
        grid=(t // tm, d // tn),
        in_specs=in_specs,
        out_specs=pl.BlockSpec((tm, tn), lambda i, j: (i, j)),
        compiler_params=_params("arbitrary", "arbitrary"),
        name="matmul_residual",
    )(*args)


def _router_kernel(x_ref, g_ref, sc_ref, sh_ref, rw_ref, rb_ref,
                   hb_ref, ti_ref, tw_ref, rank_ref, cnt_ref, *, tm):
    h = _rmsnorm_mod(x_ref[...], g_ref[...], sc_ref[0], sh_ref[0])
    hb_ref[...] = h.astype(BF16)
    logits = lax.dot_general(rw_ref[...], h, (((1,), (1,)), ((), ())),
                             precision=lax.Precision.HIGHEST, preferred_element_type=F32) + rb_ref[...]
    e_iota = lax.broadcasted_iota(I32, logits.shape, 0)
    work = logits
    ids, vals = [], []
    multi_hot = jnp.zeros(logits.shape, F32)
    for _ in range(TOP_K):
        m = jnp.max(work, axis=0, keepdims=True)
        idx = jnp.min(jnp.where(work == m, e_iota, N_EXPERTS), axis=0, keepdims=True)
        sel = e_iota == idx
        work = jnp.where(sel, -jnp.inf, work)
        multi_hot = multi_hot + jnp.where(sel, 1.0, 0.0)
        ids.append(idx)
        vals.append(m)
    ex = [jnp.exp(v - vals[0]) for v in vals]
    den = ex[0] + ex[1] + ex[2] + ex[3]
    r = lax.broadcasted_iota(I32, (tm, tm), 0)
    c = lax.broadcasted_iota(I32, (tm, tm), 1)
    before = jnp.where(r < c, 1.0, 0.0).astype(BF16)
    earlier = _dot(multi_hot.astype(BF16), before)
    for k in range(TOP_K):
        ti_ref[pl.ds(k, 1), :] = ids[k]
        tw_ref[pl.ds(k, 1), :] = ex[k] / den
        rank_ref[pl.ds(k, 1), :] = jnp.sum(jnp.where(e_iota == ids[k], earlier, 0.0),
                                           axis=0, keepdims=True).astype(I32)
    cnt_ref[0] = jnp.sum(multi_hot, axis=1, keepdims=True).astype(I32)


def _router(x, g, sc, sh, rw_t, rb, *, tm):
    t, d = x.shape
    ns, r, _ = sc.shape
    nt = t // tm
    tps = nt // ns
    mod = pl.BlockSpec((1, r if r == 1 else tm, d), lambda i: (i // tps, 0 if r == 1 else i % tps, 0))
    lane4 = pl.BlockSpec((TOP_K, tm), lambda i: (0, i))
    return pl.pallas_call(
        functools.partial(_router_kernel, tm=tm),
        out_shape=[jax.ShapeDtypeStruct((t, d), BF16), jax.ShapeDtypeStruct((TOP_K, t), I32),
                   jax.ShapeDtypeStruct((TOP_K, t), F32), jax.ShapeDtypeStruct((TOP_K, t), I32),
                   jax.ShapeDtypeStruct((nt, N_EXPERTS, 1), I32)],
        grid=(nt,),
        in_specs=[pl.BlockSpec((tm, d), lambda i: (i, 0)), pl.BlockSpec((1, d), lambda i: (0, 0)), mod, mod,
                  pl.BlockSpec((N_EXPERTS, d), lambda i: (0, 0)), pl.BlockSpec((N_EXPERTS, 1), lambda i: (0, 0))],
        out_specs=[pl.BlockSpec((tm, d), lambda i: (i, 0)), lane4, lane4, lane4,
                   pl.BlockSpec((1, N_EXPERTS, 1), lambda i: (i, 0, 0))],
        compiler_params=_params("arbitrary"),
        name="moe_router",
    )(x, g.reshape(1, d), sc, sh, rw_t, rb.reshape(N_EXPERTS, 1))


def _run_copies(sizes_ref, offs_ref, tile, make_copy):
    def start_one(e, local):
        n = pl.multiple_of(sizes_ref[tile * N_EXPERTS + e], MOE_ROW_ALIGN)
        off = pl.multiple_of(offs_ref[tile * N_EXPERTS + e], MOE_ROW_ALIGN)
        loc = pl.multiple_of(local, MOE_ROW_ALIGN)

        @pl.when(n > 0)
        def _():
            make_copy(loc, off, n).start()
        return local + n

    lax.fori_loop(0, N_EXPERTS, start_one, 0)

    def wait_one(e, local):
        n = pl.multiple_of(sizes_ref[tile * N_EXPERTS + e], MOE_ROW_ALIGN)
        off = pl.multiple_of(offs_ref[tile * N_EXPERTS + e], MOE_ROW_ALIGN)
        loc = pl.multiple_of(local, MOE_ROW_ALIGN)

        @pl.when(n > 0)
        def _():
            make_copy(loc, off, n).wait()
        return local + n

    lax.fori_loop(0, N_EXPERTS, wait_one, 0)


def _zero_fill_unused(gap_ref, nu_ref, xb_hbm, zero_scr, sem, *, bm, nb):
    zero_scr[...] = jnp.zeros(zero_scr.shape, F32)

    def gap_copy(e):
        n = pl.multiple_of(gap_ref[N_EXPERTS + e], MOE_ROW_ALIGN)
        off = pl.multiple_of(gap_ref[e], MOE_ROW_ALIGN)
        return n, pltpu.make_async_copy(zero_scr.at[pl.ds(0, n)], xb_hbm.at[pl.ds(off, n)], sem)

    def blk_copy(b):
        return pltpu.make_async_copy(zero_scr, xb_hbm.at[pl.ds(pl.multiple_of(b * bm, bm), bm)], sem)

    def each_gap(action):
        def body(e, carry):
            n, cp = gap_copy(e)

            @pl.when(n > 0)
            def _():
                action(cp)
            return carry
        lax.fori_loop(0, N_EXPERTS, body, 0)

    def each_blk(action):
        def body(b, carry):
            action(blk_copy(b))
            return carry
        lax.fori_loop(nu_ref[0], nb, body, 0)

    each_gap(lambda cp: cp.start())
    each_blk(lambda cp: cp.start())
    each_gap(lambda cp: cp.wait())
    each_blk(lambda cp: cp.wait())


def _dispatch_kernel(sizes_ref, offs_ref, gap_ref, nu_ref, hb_ref, pos_ref, xb_hbm, sorted_scr, zero_scr, sem,
                     *, rows, bm, nb):
    tile = pl.program_id(0)
    tm = hb_ref.shape[0]
    r_iota = lax.broadcasted_iota(I32, (rows, tm), 0)
    hit = r_iota == pos_ref[pl.ds(0, 1), :]
    for k in range(1, TOP_K):
        hit = hit | (r_iota == pos_ref[pl.ds(k, 1), :])
    perm = jnp.where(hit, 1.0, 0.0).astype(BF16)
    sorted_scr[...] = _dot(perm, hb_ref[...])

    def make_copy(loc, off, n):
        return pltpu.make_async_copy(sorted_scr.at[pl.ds(loc, n)], xb_hbm.at[pl.ds(off, n)], sem)

    _run_copies(sizes_ref, offs_ref, tile, make_copy)

    @pl.when(tile == pl.num_programs(0) - 1)
    def _():
        _zero_fill_unused(gap_ref, nu_ref, xb_hbm, zero_scr, sem, bm=bm, nb=nb)


def _dispatch(hb, pos_t, sizes, offs, gaps, n_used, *, tm, rows, cap, bm):
    t, d = hb.shape
    grid_spec = pltpu.PrefetchScalarGridSpec(
        num_scalar_prefetch=4,
        grid=(t // tm,),
        in_specs=[pl.BlockSpec((tm, d), lambda i, *_: (i, 0)),
                  pl.BlockSpec((TOP_K, tm), lambda i, *_: (0, i))],
        out_specs=pl.BlockSpec(memory_space=pl.ANY),
        scratch_shapes=[pltpu.VMEM((rows, d), F32), pltpu.VMEM((bm, d), F32), pltpu.SemaphoreType.DMA(())])
    return pl.pallas_call(
        functools.partial(_dispatch_kernel, rows=rows, bm=bm, nb=cap // bm),
        out_shape=jax.ShapeDtypeStruct((cap, d), F32),
        grid_spec=grid_spec,
        compiler_params=_params("arbitrary"),
        name="moe_dispatch",
    )(sizes, offs, gaps, n_used, hb, pos_t)


def _expert_kernel(be_ref, bv_ref, nu_ref, x_ref, wgu_ref, bgu_ref, wdn_ref, bdn_ref, y_ref,
                   wgu_scr, wdn_scr, *, d_ff):
    b = pl.program_id(0)
    prev_e = be_ref[jnp.maximum(b - 1, 0)]

    @pl.when((b == 0) | (be_ref[b] != prev_e))
    def _():
        wgu_scr[...] = wgu_ref[0].astype(BF16)
        wdn_scr[...] = wdn_ref[0].astype(BF16)

    @pl.when(b < nu_ref[0])
    def _():
        x = x_ref[...]
        row = lax.broadcasted_iota(I32, x.shape, 0)
        x = jnp.where(row < bv_ref[b], x, 0.0).astype(BF16)
        hg = _dot(x, wgu_scr[...]) + bgu_ref[0]
        g = jnp.minimum(hg[:, :d_ff], SWIGLU_LIMIT)
        lin = jnp.clip(hg[:, d_ff:], -SWIGLU_LIMIT, SWIGLU_LIMIT)
        act = g * _sigmoid(SWIGLU_ALPHA * g) * (lin + 1.0)
        y_ref[...] = _dot(act.astype(BF16), wdn_scr[...]) + bdn_ref[0]

    @pl.when(b >= nu_ref[0])
    def _():
        y_ref[...] = jnp.zeros(y_ref.shape, F32)


def _experts(xb, blk_e, blk_valid, n_used, w_gu, b_gu, w_dn, b_dn, *, bm):
    cap, d = xb.shape
    n_e, _, two_ff = w_gu.shape
    d_ff = two_ff // 2
    nb = cap // bm
    used = lambda b, nu: jnp.minimum(b, nu[0] - 1)
    grid_spec = pltpu.PrefetchScalarGridSpec(
        num_scalar_prefetch=3,
        grid=(nb,),
        in_specs=[pl.BlockSpec((bm, d), lambda b, be, bv, nu: (used(b, nu), 0)),
                  pl.BlockSpec((1, d, two_ff), lambda b, be, bv, nu: (be[b], 0, 0)),
                  pl.BlockSpec((1, 1, two_ff), lambda b, be, bv, nu: (be[b], 0, 0)),
                  pl.BlockSpec((1, d_ff, d), lambda b, be, bv, nu: (be[b], 0, 0)),
                  pl.BlockSpec((1, 1, d), lambda b, be, bv, nu: (be[b], 0, 0))],
        out_specs=pl.BlockSpec((bm, d), lambda b, be, bv, nu: (b, 0)),
        scratch_shapes=[pltpu.VMEM((d, two_ff), BF16), pltpu.VMEM((d_ff, d), BF16)])
    return pl.pallas_call(
        functools.partial(_expert_kernel, d_ff=d_ff),
        out_shape=jax.ShapeDtypeStruct((cap, d), F32),
        grid_spec=grid_spec,
        compiler_params=_params("arbitrary"),
        name="moe_experts",
    )(blk_e, blk_valid, n_used, xb, w_gu, b_gu.reshape(n_e, 1, two_ff), w_dn, b_dn.reshape(n_e, 1, d))


def _combine_kernel(sizes_ref, offs_ref, x_ref, gate_ref, pos_ref, tw_ref, yb_hbm, o_ref, ys_scr, sem, *, rows):
    tile = pl.program_id(0)
    tm = x_ref.shape[0]

    @pl.when(tile == 0)
    def _():
        ys_scr[...] = jnp.zeros(ys_scr.shape, F32)

    def make_copy(loc, off, n):
        return pltpu.make_async_copy(yb_hbm.at[pl.ds(off, n)], ys_scr.at[pl.ds(loc, n)], sem)

    _run_copies(sizes_ref, offs_ref, tile, make_copy)

    r_iota = lax.broadcasted_iota(I32, (tm, rows), 1)
    wmat = jnp.zeros((tm, rows), F32)
    for k in range(TOP_K):
        wmat = wmat + jnp.where(r_iota == pos_ref[:, pl.ds(k, 1)], tw_ref[:, pl.ds(k, 1)], 0.0)
    y = _dot(wmat.astype(BF16), ys_scr[...].astype(BF16))
    o_ref[...] = x_ref[...] + gate_ref[0] * y


def _combine(x, gate, pos, tw, yb, sizes, offs, *, tm, rows):
    t, d = x.shape
    ns, r, _ = gate.shape
    nt = t // tm
    tps = nt // ns
    mod = pl.BlockSpec((1, r if r == 1 else tm, d),
                       lambda i, s, o: (i // tps, 0 if r == 1 else i % tps, 0))
    col4 = pl.BlockSpec((tm, TOP_K), lambda i, s, o: (i, 0))
    grid_spec = pltpu.PrefetchScalarGridSpec(
        num_scalar_prefetch=2,
        grid=(nt,),
        in_specs=[pl.BlockSpec((tm, d), lambda i, s, o: (i, 0)), mod, col4, col4,
                  pl.BlockSpec(memory_space=pl.ANY)],
        out_specs=pl.BlockSpec((tm, d), lambda i, s, o: (i, 0)),
        scratch_shapes=[pltpu.VMEM((rows, d), F32), pltpu.SemaphoreType.DMA(())])
    return pl.pallas_call(
        functools.partial(_combine_kernel, rows=rows),
        out_shape=jax.ShapeDtypeStruct((t, d), F32),
        grid_spec=grid_spec,
        compiler_params=_params("arbitrary"),
        name="moe_combine",
    )(sizes, offs, x, gate, pos, tw, yb)


def _round_up(x, m):
    return (x + m - 1) // m * m


def _moe_layer(x, g, sc, sh, gate, rw_t, rb, w_gu, b_gu, w_dn, b_dn, *, tm, bm):
    t, d = x.shape
    nt = t // tm
    hb, top_i, top_w, rank, cnt = _router(x, g, sc, sh, rw_t, rb, tm=tm)
    cnt = cnt.reshape(nt, N_EXPERTS)
    sizes = _round_up(cnt, MOE_ROW_ALIGN)
    local_base = jnp.cumsum(sizes, axis=1) - sizes
    region = _round_up(jnp.sum(sizes, axis=0), bm)
    region_base = jnp.cumsum(region) - region
    offs = region_base[None, :] + jnp.cumsum(sizes, axis=0) - sizes
    tile_of = jnp.arange(t, dtype=I32) // tm
    pos_t = local_base[tile_of[None, :], top_i] + rank
    rows = tm * TOP_K + N_EXPERTS * MOE_ROW_ALIGN
    cap = _round_up(t * TOP_K + nt * N_EXPERTS * (MOE_ROW_ALIGN - 1), bm) + N_EXPERTS * bm
    nb = cap // bm
    blk_start = jnp.arange(nb, dtype=I32) * bm
    region_end = region_base + region
    n_used = (region_end[-1] // bm).astype(I32)
    blk_e = jnp.minimum(jnp.searchsorted(region_end, blk_start, side="right"), N_EXPERTS - 1).astype(I32)
    last_e = blk_e[jnp.maximum(n_used - 1, 0)]
    blk_e = jnp.where(blk_start < region_end[-1], blk_e, last_e)
    rows_e = jnp.sum(sizes, axis=0)
    blk_valid = jnp.clip(region_base[blk_e] + rows_e[blk_e] - blk_start, 0, bm).astype(I32)
    sizes_f = sizes.reshape(-1).astype(I32)
    offs_f = offs.reshape(-1).astype(I32)
    gaps = jnp.concatenate([region_base + rows_e, region - rows_e]).astype(I32)
    n_used = n_used.reshape(1)
    xb = _dispatch(hb, pos_t.astype(I32), sizes_f, offs_f, gaps, n_used, tm=tm, rows=rows, cap=cap, bm=bm)
    yb = _experts(xb, blk_e, blk_valid, n_used, w_gu, b_gu, w_dn, b_dn, bm=bm)
    return _combine(x, gate, pos_t.T.astype(I32), top_w.T, yb, sizes_f, offs_f, tm=tm, rows=rows)


def _trunk(x, mods, pos, attend, sc_prev, conf_prev, p, *, b, s_len, prompt):
    t, d = x.shape
    depth = p["ada_w"].shape[0]
    tm = min(1024, s_len) if prompt else t
    moe_tm, moe_bm = (256, 256) if prompt else (t, 128)
    k_rows, v_rows, sc_out, conf_out = [], [], [], []
    for l in range(depth):
        sh1, sc1, g1, sh2, sc2, g2 = mods[l]
        if l % 2 == 0:
            li = l // 2
            lam_init = 0.8 - 0.6 * math.exp(-0.3 * l)
            z = _norm_mod_matmul(x, p["norm_mix_g"][l], sc1, sh1, p["hy_w_in"][li], None, tm=tm, tn=768)
            tables = _rope_tables(pos)
            kr, vr, qb, kb, vb = _qk_post(z, tables, p["q_norm_g"][li], p["k_norm_g"][li], tm=min(512, t), d=d)
            lams = (p["lambda_q1"][li], p["lambda_k1"][li], p["lambda_q2"][li], p["lambda_k2"][li])
            o = attend(qb, kb, vb, kr, vr, lams, p["subln_g"][li], li, lam_init)
            c_sc = p["sconv_w"].shape[-1]
            ucol, ccol, bcol = 3 * d // c_sc + 2, 3 * d // c_sc + 1, 3 * d // c_sc
            if prompt:
                o_sc, tail = _conv_prompt(z, ucol, sc_prev[li], p["sconv_w"][li], b=b, s_len=s_len, c=c_sc,
                                          tt=512, gate_cols=(ccol, bcol))
                sc_new = tail[:, SUBLANES - (SC_KERNEL - 1):]
            else:
                tmaj = lambda col: z[:, col * c_sc:(col + 1) * c_sc].reshape(b, s_len, c_sc).transpose(1, 0, 2)
                o_t, cu_t = _conv_sample(tmaj(ucol), sc_prev[li].transpose(1, 0, 2), p["sconv_w"][li],
                                         gates=(tmaj(ccol), tmaj(bcol)))
                o_sc = o_t.transpose(1, 0, 2).reshape(t, c_sc)
                sc_new = jnp.concatenate([sc_prev[li], cu_t.transpose(1, 0, 2)], axis=1)[:, -(SC_KERNEL - 1):]
            w_out = p["hy_w_out"][li]
            x = _matmul_residual([o, o_sc], [w_out[:d], w_out[d:]], None, x, g1, tm=tm, tn=512)
            k_rows.append(kr)
            v_rows.append(vr)
            sc_out.append(sc_new)
        else:
            ci = l // 2
            u = _norm_mod_matmul(x, p["norm_mix_g"][l], sc1, sh1, p["conf_w1"][ci], p["conf_b1"][ci],
                                 tm=tm, tn=512, glu=True)
            ln_params = (p["conf_dw_b"][ci], p["conf_ln_g"][ci], p["conf_ln_b"][ci])
            if prompt:
                y = _conv_prompt(u, 0, conf_prev[ci], p["conf_dw"][ci], b=b, s_len=s_len, c=d, tt=512,
                                 ln_params=ln_params)
                cf_new = u.reshape(b, s_len, d)[:, s_len - (CONF_KERNEL - 1):]
            else:
                u_t = u.reshape(b, s_len, d).transpose(1, 0, 2)
                y_t, _ = _conv_sample(u_t, conf_prev[ci].transpose(1, 0, 2), p["conf_dw"][ci], ln_params=ln_params)
                y = y_t.transpose(1, 0, 2).reshape(t, d)
                cf_new = jnp.concatenate([conf_prev[ci], u.reshape(b, s_len, d)], axis=1)[:, -(CONF_KERNEL - 1):]
            x = _matmul_residual([y], [p["conf_w2"][ci]], p["conf_b2"][ci], x, g1, tm=tm, tn=512)
            conf_out.append(cf_new)
        x = _moe_layer(x, p["norm_ffn_g"][l], sc2, sh2, g2, p["router_w_t"][l], p["router_b"][l],
                       p["moe_w_gu"][l], p["moe_b_gu"][l], p["moe_w_dn"][l], p["moe_b_dn"][l],
                       tm=moe_tm, bm=moe_bm)
    n_heads = d // V_DIM
    shape_kv = lambda a: a.reshape(b, s_len, n_heads, V_DIM)
    return (x.reshape(b, s_len, d), jnp.stack([shape_kv(a) for a in k_rows]), jnp.stack([shape_kv(a) for a in v_rows]),
            jnp.stack(sc_out), jnp.stack(conf_out))


def kernel(x_prompt, x_sample, cache_k, cache_v, state_sconv, state_conf, page_table, c_prompt, c_sample,
           ada_w, ada_b, norm_mix_g, norm_ffn_g, hy_w_in, hy_w_out, q_norm_g, k_norm_g,
           lambda_q1, lambda_k1, lambda_q2, lambda_k2, subln_g, sconv_w,
           conf_w1, conf_b1, conf_dw, conf_dw_b, conf_ln_g, conf_ln_b, conf_w2, conf_b2,
           router_w, router_b, moe_w_gu, moe_b_gu, moe_w_dn, moe_b_dn):
    bp, s_len, d = x_prompt.shape
    bd, t_new, _ = x_sample.shape
    depth = ada_w.shape[0]
    p = {"ada_w": ada_w, "norm_mix_g": norm_mix_g, "norm_ffn_g": norm_ffn_g,
         "hy_w_in": hy_w_in.astype(BF16), "hy_w_out": hy_w_out.astype(BF16),
         "q_norm_g": q_norm_g, "k_norm_g": k_norm_g,
         "lambda_q1": lambda_q1, "lambda_k1": lambda_k1, "lambda_q2": lambda_q2, "lambda_k2": lambda_k2,
         "subln_g": subln_g, "sconv_w": sconv_w,
         "conf_w1": conf_w1.astype(BF16), "conf_b1": conf_b1, "conf_dw": conf_dw, "conf_dw_b": conf_dw_b,
         "conf_ln_g": conf_ln_g, "conf_ln_b": conf_ln_b, "conf_w2": conf_w2.astype(BF16), "conf_b2": conf_b2,
         "router_w_t": jnp.swapaxes(router_w, 1, 2), "router_b": router_b,
         "moe_w_gu": moe_w_gu, "moe_b_gu": moe_b_gu, "moe_w_dn": moe_w_dn, "moe_b_dn": moe_b_dn}

    mod_all = _ada_mod(jnp.concatenate([c_prompt, c_sample], axis=0), ada_w, ada_b)
    mods_p, mods_s = [], []
    for l in range(depth):
        parts = [mod_all[l, :, i * d:(i + 1) * d] for i in range(6)]
        mods_p.append([m[:bp].reshape(bp, 1, d) for m in parts])
        mods_s.append([jnp.repeat(m[bp:], t_new, axis=0).reshape(1, bd * t_new, d) for m in parts])

    past_len = page_table.shape[1] * PAGE_SIZE
    pos_prompt = jnp.arange(s_len, dtype=I32)
    pos_sample = jnp.tile(past_len + jnp.arange(t_new, dtype=I32), bd)

    def attend_prompt(qb, kb, vb, kr, vr, lams, sg, li, lam_init):
        return _flash_prompt(qb, kb, vb, lams, sg, b=bp, s_len=s_len, d=d, lam_init=lam_init, tq=min(512, s_len))

    def attend_sample(qb, kb, vb, kr, vr, lams, sg, li, lam_init):
        shp = (bd, t_new, d)
        o = _decode_attn(qb.astype(F32).reshape(shp), kr.reshape(shp), vr.reshape(shp), cache_k, cache_v, li,
                         page_table, lams, sg, lam_init=lam_init, pages=4)
        return o.reshape(bd * t_new, d)

    n_even, n_odd = (depth + 1) // 2, depth // 2
    sc_zero = jnp.zeros((n_even, bp, SC_KERNEL - 1, sconv_w.shape[-1]), F32)
    conf_zero = jnp.zeros((n_odd, bp, CONF_KERNEL - 1, conf_dw.shape[-1]), F32)
    y_p, k_p, v_p, sc_p, cf_p = _trunk(x_prompt.reshape(bp * s_len, d), mods_p, pos_prompt, attend_prompt,
                                       sc_zero, conf_zero, p, b=bp, s_len=s_len, prompt=True)
    y_s, k_s, v_s, sc_s, cf_s = _trunk(x_sample.reshape(bd * t_new, d), mods_s, pos_sample, attend_sample,
                                       state_sconv, state_conf, p, b=bd, s_len=t_new, prompt=False)
    return (y_p, y_s, k_p, v_p, sc_p, cf_p, k_s, v_s, sc_s, cf_s)
```

```python
import functools
import math

import jax
import jax.numpy as jnp
from jax import lax
from jax.experimental import pallas as pl
from jax.experimental.pallas import tpu as pltpu

F32 = jnp.float32
BF16 = jnp.bfloat16
I32 = jnp.int32

EPS = 1e-6
HEAD_DIM = 64
V_DIM = 2 * HEAD_DIM
ROPE_DIM = HEAD_DIM // 4
ROPE_THETA = 500000.0
SC_KERNEL = 3
CONF_KERNEL = 31
N_EXPERTS = 32
TOP_K = 4
SWIGLU_LIMIT = 7.0
SWIGLU_ALPHA = 1.702
PAGE_SIZE = 128
Q_SCALE = HEAD_DIM ** -0.5 * math.log2(math.e)

VMEM_LIMIT_BYTES = 56 * 1024 * 1024
SUBLANES = 8
MOE_ROW_ALIGN = 8


def _params(*sem):
    return pltpu.CompilerParams(dimension_semantics=sem, vmem_limit_bytes=VMEM_LIMIT_BYTES)


def _sigmoid(x):
    return 1.0 / (1.0 + jnp.exp(-x))


def _dot(a, b):
    return jnp.dot(a, b, preferred_element_type=F32)


def _dot_nt(a, b):
    return lax.dot_general(a, b, (((1,), (1,)), ((), ())), preferred_element_type=F32)


def _rmsnorm_mod(x, g, sc, sh):
    ms = jnp.mean(x * x, axis=-1, keepdims=True)
    return x * lax.rsqrt(ms + EPS) * g * (1.0 + sc) + sh


def _ada_kernel(c_ref, w_ref, b_ref, o_ref):
    c = c_ref[...]
    s = c * _sigmoid(c)
    o_ref[0] = jnp.dot(s, w_ref[0], precision=lax.Precision.HIGHEST, preferred_element_type=F32) + b_ref[0]


def _ada_mod(c_all, ada_w, ada_b):
    depth, d, n = ada_w.shape
    r = c_all.shape[0]
    tn = 1536
    return pl.pallas_call(
        _ada_kernel,
        out_shape=jax.ShapeDtypeStruct((depth, r, n), F32),
        grid=(depth, n // tn),
        in_specs=[pl.BlockSpec((r, d), lambda l, j: (0, 0)),
                  pl.BlockSpec((1, d, tn), lambda l, j: (l, 0, j)),
                  pl.BlockSpec((1, 1, tn), lambda l, j: (l, 0, j))],
        out_specs=pl.BlockSpec((1, r, tn), lambda l, j: (l, 0, j)),
        compiler_params=_params("arbitrary", "arbitrary"),
        name="ada_mod",
    )(c_all, ada_w, ada_b.reshape(depth, 1, n))


def _nmm_kernel(*refs, glu, has_bias):
    x_ref, g_ref, sc_ref, sh_ref = refs[:4]
    rest = list(refs[4:])
    w_refs = [rest.pop(0) for _ in range(2 if glu else 1)]
    b_refs = [rest.pop(0) for _ in range((2 if glu else 1) if has_bias else 0)]
    o_ref, h_scr = rest

    @pl.when(pl.program_id(1) == 0)
    def _():
        h = _rmsnorm_mod(x_ref[...], g_ref[...], sc_ref[0], sh_ref[0])
        h_scr[...] = h.astype(BF16)

    h = h_scr[...]
    accs = []
    for i, w_ref in enumerate(w_refs):
        a = _dot(h, w_ref[...])
        if has_bias:
            a = a + b_refs[i][...]
        accs.append(a)
    if glu:
        o_ref[...] = (accs[0] * _sigmoid(accs[1])).astype(o_ref.dtype)
    else:
        o_ref[...] = accs[0].astype(o_ref.dtype)


def _norm_mod_matmul(x, g, sc, sh, w, bias, *, tm, tn, glu=False, out_dtype=F32):
    t, d = x.shape
    n_out = w.shape[1] // 2 if glu else w.shape[1]
    ns, r, _ = sc.shape
    tps = (t // tm) // ns
    nj = n_out // tn
    in_specs = [pl.BlockSpec((tm, d), lambda i, j: (i, 0)),
                pl.BlockSpec((1, d), lambda i, j: (0, 0)),
                pl.BlockSpec((1, r, d), lambda i, j: (i // tps, 0, 0)),
                pl.BlockSpec((1, r, d), lambda i, j: (i // tps, 0, 0)),
                pl.BlockSpec((d, tn), lambda i, j: (0, j))]
    args = [x, g.reshape(1, d), sc, sh, w]
    if glu:
        in_specs.append(pl.BlockSpec((d, tn), lambda i, j: (0, j + nj)))
        args.append(w)
    if bias is not None:
        b2 = bias.reshape(1, -1)
        in_specs.append(pl.BlockSpec((1, tn), lambda i, j: (0, j)))
        args.append(b2)
        if glu:
            in_specs.append(pl.BlockSpec((1, tn), lambda i, j: (0, j + nj)))
            args.append(b2)
    return pl.pallas_call(
        functools.partial(_nmm_kernel, glu=glu, has_bias=bias is not None),
        out_shape=jax.ShapeDtypeStruct((t, n_out), out_dtype),
        grid=(t // tm, nj),
        in_specs=in_specs,
        out_specs=pl.BlockSpec((tm, tn), lambda i, j: (i, j)),
        scratch_shapes=[pltpu.VMEM((tm, d), BF16)],
        compiler_params=_params("arbitrary", "arbitrary"),
        name="norm_mod_matmul",
    )(*args)


def _group_mean_square(xh):
    r = lax.broadcasted_iota(I32, (V_DIM, V_DIM), 0) // HEAD_DIM
    c = lax.broadcasted_iota(I32, (V_DIM, V_DIM), 1) // HEAD_DIM
    ones_blk = jnp.where(r == c, 1.0, 0.0).astype(BF16)
    x2 = xh * xh
    hi = x2.astype(BF16)
    lo = (x2 - hi.astype(F32)).astype(BF16)
    return (_dot(hi, ones_blk) + _dot(lo, ones_blk)) * (1.0 / HEAD_DIM)


def _qk_post_kernel(q_ref, k_ref, v_ref, cos_ref, s1_ref, s2_ref, qg_ref, kg_ref,
                    krow_ref, vrow_ref, qb_ref, kb_ref, vb_ref, *, n_heads):
    cos, s1, s2 = cos_ref[...], s1_ref[...], s2_ref[...]

    def norm_rope(xh, g):
        y = xh * lax.rsqrt(_group_mean_square(xh) + EPS) * g
        return y * cos + pltpu.roll(y, V_DIM - ROPE_DIM // 2, 1) * s1 + pltpu.roll(y, ROPE_DIM // 2, 1) * s2

    for h in range(n_heads):
        sl = slice(h * V_DIM, (h + 1) * V_DIM)
        qh = norm_rope(q_ref[:, sl], qg_ref[...])
        kh = norm_rope(k_ref[:, sl], kg_ref[...])
        krow_ref[:, sl] = kh
        kb_ref[:, sl] = kh.astype(BF16)
        qb_ref[:, sl] = (qh * Q_SCALE).astype(BF16)
    v = v_ref[...]
    vrow_ref[...] = v
    vb_ref[...] = v.astype(BF16)


def _rope_tables(pos):
    half = ROPE_DIM // 2
    inv_freq = jnp.exp(-math.log(ROPE_THETA) * jnp.arange(half, dtype=F32) / half)
    ang = pos.astype(F32)[:, None] * inv_freq[None, :]
    cos, sin = jnp.cos(ang), jnp.sin(ang)
    p = pos.shape[0]
    ones = jnp.ones((p, HEAD_DIM - ROPE_DIM), F32)
    zeros = jnp.zeros((p, HEAD_DIM - ROPE_DIM), F32)
    z8 = jnp.zeros((p, half), F32)
    c64 = jnp.concatenate([cos, cos, ones], axis=1)
    s1_64 = jnp.concatenate([-sin, z8, zeros], axis=1)
    s2_64 = jnp.concatenate([z8, sin, zeros], axis=1)
    two = lambda a: jnp.concatenate([a, a], axis=1)
    return two(c64), two(s1_64), two(s2_64)


def _qk_post(z, tables, q_g, k_g, *, tm, d):
    t = z.shape[0]
    n_heads = d // V_DIM
    p = tables[0].shape[0]
    npb = p // tm
    row = lambda c: pl.BlockSpec((tm, d), lambda i: (i, c))
    tab = pl.BlockSpec((tm, V_DIM), lambda i: (i % npb, 0))
    gspec = pl.BlockSpec((1, V_DIM), lambda i: (0, 0))
    out = pl.BlockSpec((tm, d), lambda i: (i, 0))
    g2 = lambda g: jnp.concatenate([g, g]).reshape(1, V_DIM)
    return pl.pallas_call(
        functools.partial(_qk_post_kernel, n_heads=n_heads),
        out_shape=[jax.ShapeDtypeStruct((t, d), F32), jax.ShapeDtypeStruct((t, d), F32),
                   jax.ShapeDtypeStruct((t, d), BF16), jax.ShapeDtypeStruct((t, d), BF16),
                   jax.ShapeDtypeStruct((t, d), BF16)],
        grid=(t // tm,),
        in_specs=[row(0), row(1), row(2), tab, tab, tab, gspec, gspec],
        out_specs=[out, out, out, out, out],
        compiler_params=_params("arbitrary"),
        name="qk_post",
    )(z, z, z, *tables, g2(q_g), g2(k_g))


def _diff_lambda(lq1, lk1, lq2, lk2, lam_init):
    a = jnp.sum(lq1 * lk1, axis=-1, keepdims=True)
    b = jnp.sum(lq2 * lk2, axis=-1, keepdims=True)
    return jnp.exp(a) - jnp.exp(b) + lam_init


def _subln(o, g, lam_init):
    ms = jnp.mean(o * o, axis=-1, keepdims=True)
    return o * lax.rsqrt(ms + EPS) * g * (1.0 - lam_init)


def _flash_kernel(q_ref, k_ref, v_ref, lq1, lk1, lq2, lk2, sg_ref, o_ref,
                  qm_scr, m_scr, acc_scr, *, lam_init, tq, hb):
    qi, ki, nk = pl.program_id(2), pl.program_id(3), pl.num_programs(3)
    nchunk = tq // V_DIM

    @pl.when(ki == 0)
    def _():
        for h in range(hb):
            q = q_ref[:, h * V_DIM:(h + 1) * V_DIM]
            lane = lax.broadcasted_iota(I32, q.shape, 1)
            zero = jnp.zeros_like(q)
            qm_scr[h, pl.ds(0, tq), :] = jnp.where(lane < HEAD_DIM, q, zero)
            qm_scr[h, pl.ds(tq, tq), :] = jnp.where(lane >= HEAD_DIM, q, zero)
        m_scr[...] = jnp.full(m_scr.shape, -jnp.inf, F32)
        acc_scr[...] = jnp.zeros(acc_scr.shape, F32)

    def tile_update(on_diagonal):
        for h in range(hb):
            k = k_ref[:, h * V_DIM:(h + 1) * V_DIM]
            v = v_ref[:, h * V_DIM:(h + 1) * V_DIM]
            v_ext = jnp.concatenate([v, jnp.ones_like(v)], axis=1)
            s = _dot_nt(qm_scr[h], k)
            if on_diagonal:
                row = lax.broadcasted_iota(I32, s.shape, 0) % tq
                col = lax.broadcasted_iota(I32, s.shape, 1)
                s = jnp.where(col <= row, s, -jnp.inf)
            m_old = m_scr[h]
            m_new = jnp.maximum(m_old, jnp.max(s, axis=-1, keepdims=True))
            alpha = jnp.exp2(m_old - m_new)
            p = jnp.concatenate([jnp.exp2(s[:, j * V_DIM:(j + 1) * V_DIM] - m_new) for j in range(nchunk)], axis=1)
            acc_scr[h] = jnp.concatenate([alpha, alpha], axis=1) * acc_scr[h] + _dot(p.astype(BF16), v_ext)
            m_scr[h] = m_new

    @pl.when(ki < qi)
    def _():
        tile_update(False)

    @pl.when(ki == qi)
    def _():
        tile_update(True)

    @pl.when(ki == nk - 1)
    def _():
        lam = _diff_lambda(lq1[...], lk1[...], lq2[...], lk2[...], lam_init)
        for h in range(hb):
            acc = acc_scr[h]
            o = acc[:tq, :V_DIM] / acc[:tq, V_DIM:] - lam * (acc[tq:, :V_DIM] / acc[tq:, V_DIM:])
            o_ref[:, h * V_DIM:(h + 1) * V_DIM] = _subln(o, sg_ref[...], lam_init).astype(o_ref.dtype)


def _flash_prompt(qb, kb, vb, lams, subln_g, *, b, s_len, d, lam_init, tq, hb):
    n_heads = d // V_DIM
    nq = s_len // tq
    w = hb * V_DIM
    lam_spec = pl.BlockSpec((1, HEAD_DIM), lambda bi, h, qi, ki: (0, 0))
    return pl.pallas_call(
        functools.partial(_flash_kernel, lam_init=lam_init, tq=tq, hb=hb),
        out_shape=jax.ShapeDtypeStruct((b * s_len, d), BF16),
        grid=(b, n_heads // hb, nq, nq),
        in_specs=[pl.BlockSpec((tq, w), lambda bi, h, qi, ki: (bi * nq + qi, h)),
                  pl.BlockSpec((tq, w), lambda bi, h, qi, ki: (bi * nq + jnp.minimum(ki, qi), h)),
                  pl.BlockSpec((tq, w), lambda bi, h, qi, ki: (bi * nq + jnp.minimum(ki, qi), h)),
                  lam_spec, lam_spec, lam_spec, lam_spec,
                  pl.BlockSpec((1, V_DIM), lambda bi, h, qi, ki: (0, 0))],
        out_specs=pl.BlockSpec((tq, w), lambda bi, h, qi, ki: (bi * nq + qi, h)),
        scratch_shapes=[pltpu.VMEM((hb, 2 * tq, V_DIM), BF16), pltpu.VMEM((hb, 2 * tq, V_DIM), F32),
                        pltpu.VMEM((hb, 2 * tq, 2 * V_DIM), F32)],
        compiler_params=_params("arbitrary", "arbitrary", "arbitrary", "arbitrary"),
        name="flash_prompt",
    )(qb, kb, vb, *[x.reshape(1, HEAD_DIM) for x in lams], subln_g.reshape(1, V_DIM))


def _decode_kernel(pt_ref, q_ref, kn_ref, vn_ref, lq1, lk1, lq2, lk2, sg_ref, *refs,
                   lam_init, n_heads, pages, t_new):
    k_refs, v_refs = refs[:pages], refs[pages:2 * pages]
    o_ref, q_scr, m_scr, l_scr, acc_scr = refs[2 * pages:]
    ci, nc = pl.program_id(1), pl.num_programs(1)
    rows = 2 * t_new * n_heads
    row_head = lax.broadcasted_iota(I32, (rows, V_DIM), 0) // (2 * t_new)
    lane = lax.broadcasted_iota(I32, (rows, V_DIM), 1)

    @pl.when(ci == 0)
    def _():
        comp = (lax.broadcasted_iota(I32, (rows, V_DIM), 0) // t_new) % 2
        q = q_ref[0]
        q_scr[...] = jnp.where(lane // HEAD_DIM == comp, q, jnp.zeros_like(q))
        n_new = t_new * n_heads
        s = _dot_nt(q_scr[...], kn_ref[0].astype(BF16))
        r = lax.broadcasted_iota(I32, s.shape, 0)
        c = lax.broadcasted_iota(I32, s.shape, 1)
        visible = (c % n_heads == r // (2 * t_new)) & (c // n_heads <= r % t_new)
        s = jnp.where(visible, s, -jnp.inf)
        m = jnp.max(s, axis=-1, keepdims=True)
        p = jnp.exp2(s - m)
        m_scr[...] = jnp.broadcast_to(m, m_scr.shape)
        l_scr[...] = jnp.broadcast_to(jnp.sum(p, axis=-1, keepdims=True), l_scr.shape)
        acc_scr[...] = _dot(p.astype(BF16), vn_ref[0].astype(BF16))

    k2 = jnp.concatenate([r[0].astype(BF16) for r in k_refs], axis=0)
    v2 = jnp.concatenate([r[0].astype(BF16) for r in v_refs], axis=0)
    s = _dot_nt(q_scr[...], k2)
    bias = jnp.where(lane % n_heads == row_head, 0.0, -jnp.inf)
    chunks = [s[:, j * V_DIM:(j + 1) * V_DIM] + bias for j in range(s.shape[1] // V_DIM)]
    m_old = m_scr[...]
    m_new = jnp.maximum(m_old, jnp.max(functools.reduce(jnp.maximum, chunks), axis=-1, keepdims=True))
    alpha = jnp.exp2(m_old - m_new)
    ps = [jnp.exp2(ch - m_new) for ch in chunks]
    l_scr[...] = alpha * l_scr[...] + jnp.sum(functools.reduce(jnp.add, ps), axis=-1, keepdims=True)
    acc_scr[...] = alpha * acc_scr[...] + _dot(jnp.concatenate(ps, axis=1).astype(BF16), v2)
    m_scr[...] = m_new

    @pl.when(ci == nc - 1)
    def _():
        lam = _diff_lambda(lq1[...], lk1[...], lq2[...], lk2[...], lam_init)
        o = acc_scr[...] / l_scr[...]
        o = o - lam * pltpu.roll(o, rows - t_new, 0)
        o_ref[0] = _subln(o, sg_ref[...], lam_init)


def _decode_attn(qb, k_new, v_new, cache_k, cache_v, li, page_table, lams, subln_g, *, bd, t_new, lam_init, pages):
    d = qb.shape[1]
    n_heads = d // V_DIM
    n_pool = cache_k.shape[1]
    rows_per_page = PAGE_SIZE * n_heads
    ck = cache_k.reshape(-1, rows_per_page, V_DIM)
    cv = cache_v.reshape(-1, rows_per_page, V_DIM)
    n_pages = page_table.shape[1]
    nc = n_pages // pages
    rows = 2 * t_new * n_heads
    n_new = t_new * n_heads
    q2 = jnp.broadcast_to(qb.reshape(bd, t_new, n_heads, 1, V_DIM).transpose(0, 2, 3, 1, 4),
                          (bd, n_heads, 2, t_new, V_DIM)).reshape(bd, rows, V_DIM)

    def page_spec(p):
        return pl.BlockSpec((1, rows_per_page, V_DIM),
                            lambda b, c, pt: (li * n_pool + pt[b, c * pages + p], 0, 0))

    new_spec = pl.BlockSpec((1, n_new, V_DIM), lambda b, c, pt: (b, 0, 0))
    lam_spec = pl.BlockSpec((1, HEAD_DIM), lambda b, c, pt: (0, 0))
    grid_spec = pltpu.PrefetchScalarGridSpec(
        num_scalar_prefetch=1,
        grid=(bd, nc),
        in_specs=[pl.BlockSpec((1, rows, V_DIM), lambda b, c, pt: (b, 0, 0)), new_spec, new_spec,
                  lam_spec, lam_spec, lam_spec, lam_spec, pl.BlockSpec((1, V_DIM), lambda b, c, pt: (0, 0))]
                 + [page_spec(p) for p in range(pages)] + [page_spec(p) for p in range(pages)],
        out_specs=pl.BlockSpec((1, rows, V_DIM), lambda b, c, pt: (b, 0, 0)),
        scratch_shapes=[pltpu.VMEM((rows, V_DIM), BF16), pltpu.VMEM((rows, V_DIM), F32),
                        pltpu.VMEM((rows, V_DIM), F32), pltpu.VMEM((rows, V_DIM), F32)])
    o2 = pl.pallas_call(
        functools.partial(_decode_kernel, lam_init=lam_init, n_heads=n_heads, pages=pages, t_new=t_new),
        out_shape=jax.ShapeDtypeStruct((bd, rows, V_DIM), F32),
        grid_spec=grid_spec,
        compiler_params=_params("arbitrary", "arbitrary"),
        name="decode_attn",
    )(page_table, q2, k_new.reshape(bd, n_new, V_DIM), v_new.reshape(bd, n_new, V_DIM),
      *[x.reshape(1, HEAD_DIM) for x in lams], subln_g.reshape(1, V_DIM), *([ck] * pages), *([cv] * pages))
    o = o2.reshape(bd, n_heads, 2, t_new, V_DIM)[:, :, 0].transpose(0, 2, 1, 3)
    return o.reshape(bd * t_new, d).astype(BF16)


def _layernorm_silu(y, g, b):
    mu = jnp.mean(y, axis=-1, keepdims=True)
    yc = y - mu
    var = jnp.mean(yc * yc, axis=-1, keepdims=True)
    z = yc * lax.rsqrt(var + EPS) * g + b
    return z * _sigmoid(z)


def _conv_prompt_kernel(*refs, taps, halo, tt, gated, ln):
    refs = list(refs)
    u_ref, uh_ref, prev_ref = refs[:3]
    del refs[:3]
    if gated:
        gc_ref, gch_ref, gb_ref = refs[:3]
        del refs[:3]
    w_ref = refs.pop(0)
    if ln:
        cb_ref, lg_ref, lb_ref = refs[:3]
        del refs[:3]
    o_ref = refs.pop(0)
    if gated:
        tail_ref = refs.pop(0)
    ext = refs.pop(0)
    ti, nt = pl.program_id(1), pl.num_programs(1)

    cur = u_ref[...]
    if gated:
        cur = cur * gc_ref[...]
    ext[pl.ds(halo, tt), :] = cur

    @pl.when(ti == 0)
    def _():
        ext[pl.ds(0, halo), :] = prev_ref[0]

    @pl.when(ti > 0)
    def _():
        hv = uh_ref[...]
        if gated:
            hv = hv * gch_ref[...]
        ext[pl.ds(0, halo), :] = hv

    base = halo - (taps - 1)
    y = w_ref[pl.ds(0, 1), :] * ext[pl.ds(base, tt), :]
    for j in range(1, taps):
        y = y + w_ref[pl.ds(j, 1), :] * ext[pl.ds(base + j, tt), :]
    if ln:
        y = _layernorm_silu(y + cb_ref[...], lg_ref[...], lb_ref[...])
    if gated:
        y = gb_ref[...] * y

        @pl.when(ti == nt - 1)
        def _():
            tail_ref[0] = ext[pl.ds(halo + tt - SUBLANES, SUBLANES), :]
    o_ref[...] = y.astype(o_ref.dtype)


def _conv_prompt(src, ucol, prev, w, *, b, s_len, c, tt, gate_cols=None, ln_params=None):
    taps = w.shape[0]
    halo = SUBLANES if taps - 1 <= SUBLANES else 32
    nt = s_len // tt
    hpt = tt // halo
    prev_p = jnp.pad(prev, ((0, 0), (halo - (taps - 1), 0), (0, 0)))
    cur = lambda col: pl.BlockSpec((tt, c), lambda bi, ti: (bi * nt + ti, col))
    hal = lambda col: pl.BlockSpec((halo, c), lambda bi, ti: (jnp.maximum((bi * nt + ti) * hpt - 1, 0), col))
    vec = pl.BlockSpec((1, c), lambda bi, ti: (0, 0))
    in_specs = [cur(ucol), hal(ucol), pl.BlockSpec((1, halo, c), lambda bi, ti: (bi, 0, 0))]
    args = [src, src, prev_p]
    gated = gate_cols is not None
    if gated:
        cc, cb = gate_cols
        in_specs += [cur(cc), hal(cc), cur(cb)]
        args += [src, src, src]
    in_specs.append(pl.BlockSpec((taps, c), lambda bi, ti: (0, 0)))
    args.append(w)
    if ln_params is not None:
        in_specs += [vec, vec, vec]
        args += [p.reshape(1, c) for p in ln_params]
    out_shape = [jax.ShapeDtypeStruct((b * s_len, c), BF16)]
    out_specs = [pl.BlockSpec((tt, c), lambda bi, ti: (bi * nt + ti, 0))]
    if gated:
        out_shape.append(jax.ShapeDtypeStruct((b, SUBLANES, c), F32))
        out_specs.append(pl.BlockSpec((1, SUBLANES, c), lambda bi, ti: (bi, 0, 0)))
    res = pl.pallas_call(
        functools.partial(_conv_prompt_kernel, taps=taps, halo=halo, tt=tt, gated=gated, ln=ln_params is not None),
        out_shape=out_shape,
        grid=(b, nt),
        in_specs=in_specs,
        out_specs=out_specs,
        scratch_shapes=[pltpu.VMEM((halo + tt, c), F32)],
        compiler_params=_params("arbitrary", "arbitrary"),
        name="conv_prompt",
    )(*args)
    return res if gated else res[0]


def _conv_sample_kernel(*refs, taps, s_new, gated, ln):
    refs = list(refs)
    prev_ref, u_ref = refs[:2]
    del refs[:2]
    if gated:
        gc_ref, gb_ref = refs[:2]
        del refs[:2]
    w_ref = refs.pop(0)
    if ln:
        cb_ref, lg_ref, lb_ref = refs[:3]
        del refs[:3]
    o_ref, cu_ref, ext = refs
    for j in range(taps - 1):
        ext[j] = prev_ref[j]
    for s in range(s_new):
        cu = u_ref[s]
        if gated:
            cu = cu * gc_ref[s]
        cu_ref[s] = cu
        ext[taps - 1 + s] = cu
    for s in range(s_new):
        y = w_ref[0] * ext[s]
        for j in range(1, taps):
            y = y + w_ref[j] * ext[s + j]
        if ln:
            y = _layernorm_silu(y + cb_ref[...], lg_ref[...], lb_ref[...])
        if gated:
            y = gb_ref[s] * y
        o_ref[s] = y.astype(o_ref.dtype)


def _conv_sample(u, prev, w, *, gates=None, ln_params=None):
    s_new, bd, c = u.shape
    taps = w.shape[0]
    args = [prev, u]
    gated = gates is not None
    if gated:
        args += list(gates)
    args.append(w.reshape(taps, 1, c))
    if ln_params is not None:
        args += [p.reshape(1, c) for p in ln_params]
    return pl.pallas_call(
        functools.partial(_conv_sample_kernel, taps=taps, s_new=s_new, gated=gated, ln=ln_params is not None),
        out_shape=[jax.ShapeDtypeStruct((s_new, bd, c), BF16), jax.ShapeDtypeStruct((s_new, bd, c), F32)],
        scratch_shapes=[pltpu.VMEM((taps - 1 + s_new, bd, c), F32)],
        compiler_params=pltpu.CompilerParams(vmem_limit_bytes=VMEM_LIMIT_BYTES),
        name="conv_sample",
    )(*args)


def _mm_res_kernel(*refs, n_in, has_bias):
    a_refs, w_refs = refs[:n_in], refs[n_in:2 * n_in]
    rest = list(refs[2 * n_in:])
    b_ref = rest.pop(0) if has_bias else None
    x_ref, gate_ref, o_ref = rest
    acc = _dot(a_refs[0][...], w_refs[0][...])
    for a_ref, w_ref in zip(a_refs[1:], w_refs[1:]):
        acc = acc + _dot(a_ref[...], w_ref[...])
    if has_bias:
        acc = acc + b_ref[...]
    o_ref[...] = x_ref[...] + gate_ref[0] * acc


def _matmul_residual(a_list, w_list, bias, x, gate, *, tm, tn):
    t, d = x.shape
    ns, r, _ = gate.shape
    tps = (t // tm) // ns
    n_in = len(a_list)
    in_specs = [pl.BlockSpec((tm, a.shape[1]), lambda i, j: (i, 0)) for a in a_list]
    in_specs += [pl.BlockSpec((w.shape[0], tn), lambda i, j: (0, j)) for w in w_list]
    args = list(a_list) + list(w_list)
    if bias is not None:
        in_specs.append(pl.BlockSpec((1, tn), lambda i, j: (0, j)))
        args.append(bias.reshape(1, d))
    in_specs += [pl.BlockSpec((tm, tn), lambda i, j: (i, j)),
                 pl.BlockSpec((1, r, tn), lambda i, j: (i // tps, 0, j))]
    args += [x, gate]
    return pl.pallas_call(
        functools.partial(_mm_res_kernel, n_in=n_in, has_bias=bias is not None),
        out_shape=jax.ShapeDtypeStruct((t, d), F32),
        grid=(t // tm, d // tn),
        in_specs=in_specs,
        out_specs=pl.BlockSpec((tm, tn), lambda i, j: (i, j)),
        compiler_params=_params("arbitrary", "arbitrary"),
        name="matmul_residual",
    )(*args)


def _router_kernel(x_ref, g_ref, sc_ref, sh_ref, rw_ref, rb_ref,
                   hb_ref, tw_ref, pos_ref, sizes_ref, *, tm):
    h = _rmsnorm_mod(x_ref[...], g_ref[...], sc_ref[0], sh_ref[0])
    hb_ref[...] = h.astype(BF16)
    logits = lax.dot_general(rw_ref[...], h, (((1,), (1,)), ((), ())),
                             precision=lax.Precision.HIGHEST, preferred_element_type=F32) + rb_ref[...]
    e_iota = lax.broadcasted_iota(I32, logits.shape, 0)
    work = logits
    ids, vals = [], []
    multi_hot = jnp.zeros(logits.shape, F32)
    for _ in range(TOP_K):
        m = jnp.max(work, axis=0, keepdims=True)
        idx = jnp.min(jnp.where(work == m, e_iota, N_EXPERTS), axis=0, keepdims=True)
        sel = e_iota == idx
        work = jnp.where(sel, -jnp.inf, work)
        multi_hot = multi_hot + jnp.where(sel, 1.0, 0.0)
        ids.append(idx)
        vals.append(m)
    ex = [jnp.exp(v - vals[0]) for v in vals]
    den = ex[0] + ex[1] + ex[2] + ex[3]
    r = lax.broadcasted_iota(I32, (tm, tm), 0)
    c = lax.broadcasted_iota(I32, (tm, tm), 1)
    before = jnp.where(r < c, 1.0, 0.0).astype(BF16)
    earlier = _dot(multi_hot.astype(BF16), before)
    cnt = jnp.sum(multi_hot, axis=1, keepdims=True)
    sizes = jnp.ceil(cnt * (1.0 / MOE_ROW_ALIGN)) * MOE_ROW_ALIGN
    er = lax.broadcasted_iota(I32, (N_EXPERTS, N_EXPERTS), 0)
    ec = lax.broadcasted_iota(I32, (N_EXPERTS, N_EXPERTS), 1)
    lower = jnp.where(ec < er, 1.0, 0.0).astype(BF16)
    run_start = _dot(lower, jnp.broadcast_to(sizes, (N_EXPERTS, V_DIM)).astype(BF16))[:, :1]
    row_in_tile = run_start + earlier
    for k in range(TOP_K):
        tw_ref[pl.ds(k, 1), :] = ex[k] / den
        pos_ref[pl.ds(k, 1), :] = jnp.sum(jnp.where(e_iota == ids[k], row_in_tile, 0.0),
                                          axis=0, keepdims=True).astype(I32)
    sizes_ref[0] = sizes.astype(I32)


def _router(x, g, sc, sh, rw_t, rb, *, tm):
    t, d = x.shape
    ns, r, _ = sc.shape
    nt = t // tm
    tps = nt // ns
    mod = pl.BlockSpec((1, r if r == 1 else tm, d), lambda i: (i // tps, 0 if r == 1 else i % tps, 0))
    lane4 = pl.BlockSpec((TOP_K, tm), lambda i: (0, i))
    return pl.pallas_call(
        functools.partial(_router_kernel, tm=tm),
        out_shape=[jax.ShapeDtypeStruct((t, d), BF16), jax.ShapeDtypeStruct((TOP_K, t), F32),
                   jax.ShapeDtypeStruct((TOP_K, t), I32), jax.ShapeDtypeStruct((nt, N_EXPERTS, 1), I32)],
        grid=(nt,),
        in_specs=[pl.BlockSpec((tm, d), lambda i: (i, 0)), pl.BlockSpec((1, d), lambda i: (0, 0)), mod, mod,
                  pl.BlockSpec((N_EXPERTS, d), lambda i: (0, 0)), pl.BlockSpec((N_EXPERTS, 1), lambda i: (0, 0))],
        out_specs=[pl.BlockSpec((tm, d), lambda i: (i, 0)), lane4, lane4,
                   pl.BlockSpec((1, N_EXPERTS, 1), lambda i: (i, 0, 0))],
        compiler_params=_params("arbitrary"),
        name="moe_router",
    )(x, g.reshape(1, d), sc, sh, rw_t, rb.reshape(N_EXPERTS, 1))


def _run_copies(sizes_ref, offs_ref, tile, make_copy):
    def start_one(e, local):
        n = pl.multiple_of(sizes_ref[tile * N_EXPERTS + e], MOE_ROW_ALIGN)
        off = pl.multiple_of(offs_ref[tile * N_EXPERTS + e], MOE_ROW_ALIGN)
        loc = pl.multiple_of(local, MOE_ROW_ALIGN)

        @pl.when(n > 0)
        def _():
            make_copy(loc, off, n).start()
        return local + n

    lax.fori_loop(0, N_EXPERTS, start_one, 0)

    def wait_one(e, local):
        n = pl.multiple_of(sizes_ref[tile * N_EXPERTS + e], MOE_ROW_ALIGN)
        off = pl.multiple_of(offs_ref[tile * N_EXPERTS + e], MOE_ROW_ALIGN)
        loc = pl.multiple_of(local, MOE_ROW_ALIGN)

        @pl.when(n > 0)
        def _():
            make_copy(loc, off, n).wait()
        return local + n

    lax.fori_loop(0, N_EXPERTS, wait_one, 0)


def _zero_fill_unused(gap_ref, nu_ref, xb_hbm, zero_scr, sem, *, bm, nb):
    zero_scr[...] = jnp.zeros(zero_scr.shape, F32)

    def gap_copy(e):
        n = pl.multiple_of(gap_ref[N_EXPERTS + e], MOE_ROW_ALIGN)
        off = pl.multiple_of(gap_ref[e], MOE_ROW_ALIGN)
        return n, pltpu.make_async_copy(zero_scr.at[pl.ds(0, n)], xb_hbm.at[pl.ds(off, n)], sem)

    def blk_copy(b):
        return pltpu.make_async_copy(zero_scr, xb_hbm.at[pl.ds(pl.multiple_of(b * bm, bm), bm)], sem)

    def each_gap(action):
        def body(e, carry):
            n, cp = gap_copy(e)

            @pl.when(n > 0)
            def _():
                action(cp)
            return carry
        lax.fori_loop(0, N_EXPERTS, body, 0)

    def each_blk(action):
        def body(b, carry):
            action(blk_copy(b))
            return carry
        lax.fori_loop(nu_ref[0], nb, body, 0)

    each_gap(lambda cp: cp.start())
    each_blk(lambda cp: cp.start())
    each_gap(lambda cp: cp.wait())
    each_blk(lambda cp: cp.wait())


def _moe_rows(tm):
    return tm * TOP_K + N_EXPERTS * MOE_ROW_ALIGN


def _dispatch_kernel(sizes_ref, offs_ref, gap_ref, nu_ref, *refs, tile_ranges, bm, nb):
    n_groups = len(tile_ranges)
    hb_refs, pos_refs = refs[0:2 * n_groups:2], refs[1:2 * n_groups:2]
    xb_hbm, sorted_scr, zero_scr, sem = refs[2 * n_groups:]
    tile = pl.program_id(0)

    for hb_ref, pos_ref, (first, last) in zip(hb_refs, pos_refs, tile_ranges):
        @pl.when((tile >= first) & (tile < last))
        def _():
            tm = hb_ref.shape[0]
            rows = _moe_rows(tm)
            r_iota = lax.broadcasted_iota(I32, (rows, tm), 0)
            hit = r_iota == pos_ref[pl.ds(0, 1), :]
            for k in range(1, TOP_K):
                hit = hit | (r_iota == pos_ref[pl.ds(k, 1), :])
            perm = jnp.where(hit, 1.0, 0.0).astype(BF16)
            sorted_scr[pl.ds(0, rows), :] = _dot(perm, hb_ref[...])

    def make_copy(loc, off, n):
        return pltpu.make_async_copy(sorted_scr.at[pl.ds(loc, n)], xb_hbm.at[pl.ds(off, n)], sem)

    _run_copies(sizes_ref, offs_ref, tile, make_copy)

    @pl.when(tile == pl.num_programs(0) - 1)
    def _():
        _zero_fill_unused(gap_ref, nu_ref, xb_hbm, zero_scr, sem, bm=bm, nb=nb)


def _dispatch(groups, sizes, offs, gaps, n_used, *, cap, bm):
    d = groups[0][0].shape[1]
    in_specs, args, tile_ranges = [], [], []
    first = 0
    for hb, pos_t, tm in groups:
        nt = hb.shape[0] // tm
        local = lambda i, first=first, nt=nt: jnp.clip(i - first, 0, nt - 1)
        in_specs += [pl.BlockSpec((tm, d), lambda i, *_, local=local: (local(i), 0)),
                     pl.BlockSpec((TOP_K, tm), lambda i, *_, local=local: (0, local(i)))]
        args += [hb, pos_t]
        tile_ranges.append((first, first + nt))
        first += nt
    rows = max(_moe_rows(tm) for _, _, tm in groups)
    grid_spec = pltpu.PrefetchScalarGridSpec(
        num_scalar_prefetch=4,
        grid=(first,),
        in_specs=in_specs,
        out_specs=pl.BlockSpec(memory_space=pl.ANY),
        scratch_shapes=[pltpu.VMEM((rows, d), F32), pltpu.VMEM((bm, d), F32), pltpu.SemaphoreType.DMA(())])
    return pl.pallas_call(
        functools.partial(_dispatch_kernel, tile_ranges=tuple(tile_ranges), bm=bm, nb=cap // bm),
        out_shape=jax.ShapeDtypeStruct((cap, d), F32),
        grid_spec=grid_spec,
        compiler_params=_params("arbitrary"),
        name="moe_dispatch",
    )(sizes, offs, gaps, n_used, *args)


def _expert_kernel(be_ref, bv_ref, nu_ref, x_ref, wgu_ref, bgu_ref, wdn_ref, bdn_ref, y_ref,
                   wgu_scr, wdn_scr, *, d_ff):
    b = pl.program_id(0)
    prev_e = be_ref[jnp.maximum(b - 1, 0)]

    @pl.when((b == 0) | (be_ref[b] != prev_e))
    def _():
        wgu_scr[...] = wgu_ref[0, 0].astype(BF16)
        wdn_scr[...] = wdn_ref[0, 0].astype(BF16)

    @pl.when(b < nu_ref[0])
    def _():
        x = x_ref[...]
        row = lax.broadcasted_iota(I32, x.shape, 0)
        x = jnp.where(row < bv_ref[b], x, 0.0).astype(BF16)
        hg = _dot(x, wgu_scr[...]) + bgu_ref[0, 0]
        g = jnp.minimum(hg[:, :d_ff], SWIGLU_LIMIT)
        lin = jnp.clip(hg[:, d_ff:], -SWIGLU_LIMIT, SWIGLU_LIMIT)
        act = g * _sigmoid(SWIGLU_ALPHA * g) * (lin + 1.0)
        y_ref[...] = _dot(act.astype(BF16), wdn_scr[...]) + bdn_ref[0, 0]

    @pl.when(b >= nu_ref[0])
    def _():
        y_ref[...] = jnp.zeros(y_ref.shape, F32)


def _experts(xb, blk_e, blk_valid, n_used, w_gu, b_gu, w_dn, b_dn, *, layer, bm):
    cap, d = xb.shape
    depth, n_e, _, two_ff = w_gu.shape
    d_ff = two_ff // 2
    nb = cap // bm
    used = lambda b, nu: jnp.minimum(b, nu[0] - 1)
    grid_spec = pltpu.PrefetchScalarGridSpec(
        num_scalar_prefetch=3,
        grid=(nb,),
        in_specs=[pl.BlockSpec((bm, d), lambda b, be, bv, nu: (used(b, nu), 0)),
                  pl.BlockSpec((1, 1, d, two_ff), lambda b, be, bv, nu: (layer, be[b], 0, 0)),
                  pl.BlockSpec((1, 1, 1, two_ff), lambda b, be, bv, nu: (layer, be[b], 0, 0)),
                  pl.BlockSpec((1, 1, d_ff, d), lambda b, be, bv, nu: (layer, be[b], 0, 0)),
                  pl.BlockSpec((1, 1, 1, d), lambda b, be, bv, nu: (layer, be[b], 0, 0))],
        out_specs=pl.BlockSpec((bm, d), lambda b, be, bv, nu: (b, 0)),
        scratch_shapes=[pltpu.VMEM((d, two_ff), BF16), pltpu.VMEM((d_ff, d), BF16)])
    return pl.pallas_call(
        functools.partial(_expert_kernel, d_ff=d_ff),
        out_shape=jax.ShapeDtypeStruct((cap, d), F32),
        grid_spec=grid_spec,
        compiler_params=_params("arbitrary"),
        name="moe_experts",
    )(blk_e, blk_valid, n_used, xb, w_gu, b_gu.reshape(depth, n_e, 1, two_ff), w_dn, b_dn.reshape(depth, n_e, 1, d))


def _combine_kernel(sizes_ref, offs_ref, x_ref, gate_ref, pos_ref, tw_ref, yb_hbm, o_ref, ys_scr, sem,
                    *, rows, first_tile):
    tm = x_ref.shape[0]

    @pl.when(pl.program_id(0) == 0)
    def _():
        ys_scr[...] = jnp.zeros(ys_scr.shape, F32)

    def make_copy(loc, off, n):
        return pltpu.make_async_copy(yb_hbm.at[pl.ds(off, n)], ys_scr.at[pl.ds(loc, n)], sem)

    _run_copies(sizes_ref, offs_ref, pl.program_id(0) + first_tile, make_copy)

    r_iota = lax.broadcasted_iota(I32, (tm, rows), 1)
    wmat = jnp.zeros((tm, rows), F32)
    for k in range(TOP_K):
        wmat = wmat + jnp.where(r_iota == pos_ref[:, pl.ds(k, 1)], tw_ref[:, pl.ds(k, 1)], 0.0)
    y = _dot(wmat.astype(BF16), ys_scr[...].astype(BF16))
    o_ref[...] = x_ref[...] + gate_ref[0] * y


def _combine(x, gate, pos, tw, yb, sizes, offs, *, tm, first_tile):
    t, d = x.shape
    ns, r, _ = gate.shape
    nt = t // tm
    tps = nt // ns
    rows = _moe_rows(tm)
    mod = pl.BlockSpec((1, r if r == 1 else tm, d),
                       lambda i, s, o: (i // tps, 0 if r == 1 else i % tps, 0))
    col4 = pl.BlockSpec((tm, TOP_K), lambda i, s, o: (i, 0))
    grid_spec = pltpu.PrefetchScalarGridSpec(
        num_scalar_prefetch=2,
        grid=(nt,),
        in_specs=[pl.BlockSpec((tm, d), lambda i, s, o: (i, 0)), mod, col4, col4,
                  pl.BlockSpec(memory_space=pl.ANY)],
        out_specs=pl.BlockSpec((tm, d), lambda i, s, o: (i, 0)),
        scratch_shapes=[pltpu.VMEM((rows, d), F32), pltpu.SemaphoreType.DMA(())])
    return pl.pallas_call(
        functools.partial(_combine_kernel, rows=rows, first_tile=first_tile),
        out_shape=jax.ShapeDtypeStruct((t, d), F32),
        grid_spec=grid_spec,
        compiler_params=_params("arbitrary"),
        name="moe_combine",
    )(sizes, offs, x, gate, pos, tw, yb)


def _round_up(x, m):
    return (x + m - 1) // m * m


def _moe_layer(groups, g, rw_t, rb, w_gu, b_gu, w_dn, b_dn, *, layer, bm):
    routed = [_router(x, g, sc, sh, rw_t, rb, tm=tm) for x, sc, sh, _, tm in groups]
    sizes = jnp.concatenate([r[3].reshape(-1, N_EXPERTS) for r in routed], axis=0)
    nt = sizes.shape[0]
    rows_e = jnp.sum(sizes, axis=0)
    region = _round_up(rows_e, bm)
    region_end = jnp.cumsum(region)
    region_base = region_end - region
    offs = region_base[None, :] + jnp.cumsum(sizes, axis=0) - sizes
    t_all = sum(x.shape[0] for x, *_ in groups)
    cap = _round_up(t_all * TOP_K + nt * N_EXPERTS * (MOE_ROW_ALIGN - 1), bm) + N_EXPERTS * bm
    nb = cap // bm
    blk_start = jnp.arange(nb, dtype=I32) * bm
    n_used = (region_end[-1] // bm).astype(I32).reshape(1)
    blk_e = jnp.sum((region_end[None, :] <= blk_start[:, None]).astype(I32), axis=1)
    last_e = jnp.sum((region_end < region_end[-1]).astype(I32))
    blk_e = jnp.where(blk_start < region_end[-1], blk_e, last_e).astype(I32)
    onehot = blk_e[:, None] == jnp.arange(N_EXPERTS, dtype=I32)[None, :]
    rows_end = jnp.sum(jnp.where(onehot, (region_base + rows_e)[None, :], 0), axis=1)
    blk_valid = jnp.clip(rows_end - blk_start, 0, bm).astype(I32)
    sizes_f = sizes.reshape(-1).astype(I32)
    offs_f = offs.reshape(-1).astype(I32)
    gaps = jnp.concatenate([region_base + rows_e, region - rows_e]).astype(I32)
    xb = _dispatch([(r[0], r[2], grp[4]) for r, grp in zip(routed, groups)], sizes_f, offs_f, gaps, n_used,
                   cap=cap, bm=bm)
    yb = _experts(xb, blk_e, blk_valid, n_used, w_gu, b_gu, w_dn, b_dn, layer=layer, bm=bm)
    outs, first = [], 0
    for (x, _, _, gate, tm), (_, top_w, pos_t, _) in zip(groups, routed):
        outs.append(_combine(x, gate, pos_t.T, top_w.T, yb, sizes_f, offs_f, tm=tm, first_tile=first))
        first += x.shape[0] // tm
    return outs


def _hybrid_mixer(x, l, mod, pos, attend, sc_prev, p, *, b, s_len, prompt):
    t, d = x.shape
    tm = min(1024, s_len) if prompt else t
    sh1, sc1, g1 = mod
    li = l // 2
    lam_init = 0.8 - 0.6 * math.exp(-0.3 * l)
    z = _norm_mod_matmul(x, p["norm_mix_g"][l], sc1, sh1, p["hy_w_in"][li], None, tm=tm, tn=768)
    kr, vr, qb, kb, vb = _qk_post(z, _rope_tables(pos), p["q_norm_g"][li], p["k_norm_g"][li], tm=min(512, t), d=d)
    lams = (p["lambda_q1"][li], p["lambda_k1"][li], p["lambda_q2"][li], p["lambda_k2"][li])
    o = attend(qb, kb, vb, kr, vr, lams, p["subln_g"][li], li, lam_init)
    c_sc = p["sconv_w"].shape[-1]
    bcol = 3 * d // c_sc
    ccol, ucol = bcol + 1, bcol + 2
    if prompt:
        o_sc, tail = _conv_prompt(z, ucol, sc_prev[li], p["sconv_w"][li], b=b, s_len=s_len, c=c_sc,
                                  tt=min(512, s_len), gate_cols=(ccol, bcol))
        sc_new = tail[:, SUBLANES - (SC_KERNEL - 1):]
    else:
        tmaj = lambda col: z[:, col * c_sc:(col + 1) * c_sc].reshape(b, s_len, c_sc).transpose(1, 0, 2)
        o_t, cu_t = _conv_sample(tmaj(ucol), sc_prev[li].transpose(1, 0, 2), p["sconv_w"][li],
                                 gates=(tmaj(ccol), tmaj(bcol)))
        o_sc = o_t.transpose(1, 0, 2).reshape(t, c_sc)
        sc_new = jnp.concatenate([sc_prev[li], cu_t.transpose(1, 0, 2)], axis=1)[:, -(SC_KERNEL - 1):]
    w_out = p["hy_w_out"][li]
    x = _matmul_residual([o, o_sc], [w_out[:d], w_out[d:]], None, x, g1, tm=tm, tn=512)
    n_heads = d // V_DIM
    return x, (kr.reshape(b, s_len, n_heads, V_DIM), vr.reshape(b, s_len, n_heads, V_DIM), sc_new)


def _conformer_mixer(x, l, mod, conf_prev, p, *, b, s_len, prompt):
    t, d = x.shape
    tm = min(1024, s_len) if prompt else t
    sh1, sc1, g1 = mod
    ci = l // 2
    u = _norm_mod_matmul(x, p["norm_mix_g"][l], sc1, sh1, p["conf_w1"][ci], p["conf_b1"][ci], tm=tm, tn=512, glu=True)
    ln_params = (p["conf_dw_b"][ci], p["conf_ln_g"][ci], p["conf_ln_b"][ci])
    if prompt:
        y = _conv_prompt(u, 0, conf_prev[ci], p["conf_dw"][ci], b=b, s_len=s_len, c=d, tt=min(512, s_len),
                         ln_params=ln_params)
        cf_new = u.reshape(b, s_len, d)[:, s_len - (CONF_KERNEL - 1):]
    else:
        u_t = u.reshape(b, s_len, d).transpose(1, 0, 2)
        y_t, _ = _conv_sample(u_t, conf_prev[ci].transpose(1, 0, 2), p["conf_dw"][ci], ln_params=ln_params)
        y = y_t.transpose(1, 0, 2).reshape(t, d)
        cf_new = jnp.concatenate([conf_prev[ci], u.reshape(b, s_len, d)], axis=1)[:, -(CONF_KERNEL - 1):]
    x = _matmul_residual([y], [p["conf_w2"][ci]], p["conf_b2"][ci], x, g1, tm=tm, tn=512)
    return x, (cf_new,)


def _mixer(x, l, mod, pos, attend, sc_prev, conf_prev, p, *, b, s_len, prompt):
    if l % 2 == 0:
        return _hybrid_mixer(x, l, mod, pos, attend, sc_prev, p, b=b, s_len=s_len, prompt=prompt)
    return _conformer_mixer(x, l, mod, conf_prev, p, b=b, s_len=s_len, prompt=prompt)


def kernel(x_prompt, x_sample, cache_k, cache_v, state_sconv, state_conf, page_table, c_prompt, c_sample,
           ada_w, ada_b, norm_mix_g, norm_ffn_g, hy_w_in, hy_w_out, q_norm_g, k_norm_g,
           lambda_q1, lambda_k1, lambda_q2, lambda_k2, subln_g, sconv_w,
           conf_w1, conf_b1, conf_dw, conf_dw_b, conf_ln_g, conf_ln_b, conf_w2, conf_b2,
           router_w, router_b, moe_w_gu, moe_b_gu, moe_w_dn, moe_b_dn):
    bp, s_len, d = x_prompt.shape
    bd, t_new, _ = x_sample.shape
    depth = ada_w.shape[0]
    p = {"ada_w": ada_w, "norm_mix_g": norm_mix_g, "norm_ffn_g": norm_ffn_g,
         "hy_w_in": hy_w_in.astype(BF16), "hy_w_out": hy_w_out.astype(BF16),
         "q_norm_g": q_norm_g, "k_norm_g": k_norm_g,
         "lambda_q1": lambda_q1, "lambda_k1": lambda_k1, "lambda_q2": lambda_q2, "lambda_k2": lambda_k2,
         "subln_g": subln_g, "sconv_w": sconv_w,
         "conf_w1": conf_w1.astype(BF16), "conf_b1": conf_b1, "conf_dw": conf_dw, "conf_dw_b": conf_dw_b,
         "conf_ln_g": conf_ln_g, "conf_ln_b": conf_ln_b, "conf_w2": conf_w2.astype(BF16), "conf_b2": conf_b2,
         "router_w_t": jnp.swapaxes(router_w, 1, 2), "router_b": router_b,
         "moe_w_gu": moe_w_gu, "moe_b_gu": moe_b_gu, "moe_w_dn": moe_w_dn, "moe_b_dn": moe_b_dn}

    mod_all = _ada_mod(jnp.concatenate([c_prompt, c_sample], axis=0), ada_w, ada_b)
    mods_p, mods_s = [], []
    for l in range(depth):
        parts = [mod_all[l, :, i * d:(i + 1) * d] for i in range(6)]
        mods_p.append([m[:bp].reshape(bp, 1, d) for m in parts])
        mods_s.append([jnp.repeat(m[bp:], t_new, axis=0).reshape(1, bd * t_new, d) for m in parts])

    past_len = page_table.shape[1] * PAGE_SIZE
    pos_prompt = jnp.arange(s_len, dtype=I32)
    pos_sample = jnp.tile(past_len + jnp.arange(t_new, dtype=I32), bd)

    def attend_prompt(qb, kb, vb, kr, vr, lams, sg, li, lam_init):
        return _flash_prompt(qb, kb, vb, lams, sg, b=bp, s_len=s_len, d=d, lam_init=lam_init,
                             tq=min(512, s_len), hb=4)

    def attend_sample(qb, kb, vb, kr, vr, lams, sg, li, lam_init):
        return _decode_attn(qb, kr, vr, cache_k, cache_v, li, page_table, lams, sg, bd=bd, t_new=t_new,
                            lam_init=lam_init, pages=8)

    n_even, n_odd = (depth + 1) // 2, depth // 2
    sc_zero = jnp.zeros((n_even, bp, SC_KERNEL - 1, sconv_w.shape[-1]), F32)
    conf_zero = jnp.zeros((n_odd, bp, CONF_KERNEL - 1, conf_dw.shape[-1]), F32)
    xp = x_prompt.reshape(bp * s_len, d)
    xs = x_sample.reshape(bd * t_new, d)
    new_p, new_s = [], []
    for l in range(depth):
        xp, st_p = _mixer(xp, l, mods_p[l][:3], pos_prompt, attend_prompt, sc_zero, conf_zero, p,
                          b=bp, s_len=s_len, prompt=True)
        xs, st_s = _mixer(xs, l, mods_s[l][:3], pos_sample, attend_sample, state_sconv, state_conf, p,
                          b=bd, s_len=t_new, prompt=False)
        new_p.append(st_p)
        new_s.append(st_s)
        (sh2p, sc2p, g2p), (sh2s, sc2s, g2s) = mods_p[l][3:], mods_s[l][3:]
        xp, xs = _moe_layer([(xp, sc2p, sh2p, g2p, min(256, bp * s_len)), (xs, sc2s, sh2s, g2s, bd * t_new)],
                            norm_ffn_g[l], p["router_w_t"][l], router_b[l],
                            moe_w_gu, moe_b_gu, moe_w_dn, moe_b_dn, layer=l, bm=256)

    def collect(new, b, s):
        even, odd = new[0::2], new[1::2]
        stack = lambda parts: parts[0][None] if len(parts) == 1 else jnp.stack(parts)
        return (stack([e[0] for e in even]), stack([e[1] for e in even]), stack([e[2] for e in even]),
                stack([o[0] for o in odd]))

    k_p, v_p, sc_p, cf_p = collect(new_p, bp, s_len)
    k_s, v_s, sc_s, cf_s = collect(new_s, bd, t_new)
    return (xp.reshape(bp, s_len, d), xs.reshape(bd, t_new, d), k_p, v_p, sc_p, cf_p, k_s, v_s, sc_s, cf_s)
```

```python
import functools
import math

import jax
import jax.numpy as jnp
from jax import lax
from jax.experimental import pallas as pl
from jax.experimental.pallas import tpu as pltpu

F32 = jnp.float32
BF16 = jnp.bfloat16
I32 = jnp.int32

EPS = 1e-6
HEAD_DIM = 64
V_DIM = 2 * HEAD_DIM
ROPE_DIM = HEAD_DIM // 4
ROPE_THETA = 500000.0
SC_KERNEL = 3
CONF_KERNEL = 31
N_EXPERTS = 32
TOP_K = 4
SWIGLU_LIMIT = 7.0
SWIGLU_ALPHA = 1.702
PAGE_SIZE = 128
DECODE_PAGES = 8
Q_SCALE = HEAD_DIM ** -0.5 * math.log2(math.e)

VMEM_LIMIT_BYTES = 56 * 1024 * 1024
SUBLANES = 8
MOE_ROW_ALIGN = 8
MOE_BLOCK = 512
MOE_SUB_BLOCK = 256


def _params(*sem):
    return pltpu.CompilerParams(dimension_semantics=sem, vmem_limit_bytes=VMEM_LIMIT_BYTES)


def _sigmoid(x):
    return 1.0 / (1.0 + jnp.exp(-x))


def _dot(a, b):
    return jnp.dot(a, b, preferred_element_type=F32)


def _dot_nt(a, b):
    return lax.dot_general(a, b, (((1,), (1,)), ((), ())), preferred_element_type=F32)


def _rmsnorm_mod(x, g, sc, sh):
    ms = jnp.mean(x * x, axis=-1, keepdims=True)
    return x * lax.rsqrt(ms + EPS) * g * (1.0 + sc) + sh


def _ada_kernel(c_ref, w_ref, b_ref, o_ref):
    c = c_ref[...]
    s = c * _sigmoid(c)
    o_ref[0] = jnp.dot(s, w_ref[0], precision=lax.Precision.HIGHEST, preferred_element_type=F32) + b_ref[0]


def _ada_mod(c_all, ada_w, ada_b):
    depth, d, n = ada_w.shape
    r = c_all.shape[0]
    tn = 1536
    return pl.pallas_call(
        _ada_kernel,
        out_shape=jax.ShapeDtypeStruct((depth, r, n), F32),
        grid=(depth, n // tn),
        in_specs=[pl.BlockSpec((r, d), lambda l, j: (0, 0)),
                  pl.BlockSpec((1, d, tn), lambda l, j: (l, 0, j)),
                  pl.BlockSpec((1, 1, tn), lambda l, j: (l, 0, j))],
        out_specs=pl.BlockSpec((1, r, tn), lambda l, j: (l, 0, j)),
        compiler_params=_params("arbitrary", "arbitrary"),
        name="ada_mod",
    )(c_all, ada_w, ada_b.reshape(depth, 1, n))


def _nmm_kernel(*refs, glu, has_bias):
    x_ref, g_ref, sc_ref, sh_ref = refs[:4]
    rest = list(refs[4:])
    w_refs = [rest.pop(0) for _ in range(2 if glu else 1)]
    b_refs = [rest.pop(0) for _ in range((2 if glu else 1) if has_bias else 0)]
    o_ref, h_scr = rest

    @pl.when(pl.program_id(1) == 0)
    def _():
        h = _rmsnorm_mod(x_ref[...], g_ref[...], sc_ref[0], sh_ref[0])
        h_scr[...] = h.astype(BF16)

    h = h_scr[...]
    accs = []
    for i, w_ref in enumerate(w_refs):
        a = _dot(h, w_ref[...])
        if has_bias:
            a = a + b_refs[i][...]
        accs.append(a)
    if glu:
        o_ref[...] = (accs[0] * _sigmoid(accs[1])).astype(o_ref.dtype)
    else:
        o_ref[...] = accs[0].astype(o_ref.dtype)


def _norm_mod_matmul(x, g, sc, sh, w, bias, *, tm, tn, glu=False, out_dtype=F32):
    t, d = x.shape
    n_out = w.shape[1] // 2 if glu else w.shape[1]
    ns, r, _ = sc.shape
    tps = (t // tm) // ns
    nj = n_out // tn
    in_specs = [pl.BlockSpec((tm, d), lambda i, j: (i, 0)),
                pl.BlockSpec((1, d), lambda i, j: (0, 0)),
                pl.BlockSpec((1, r, d), lambda i, j: (i // tps, 0, 0)),
                pl.BlockSpec((1, r, d), lambda i, j: (i // tps, 0, 0)),
                pl.BlockSpec((d, tn), lambda i, j: (0, j))]
    args = [x, g.reshape(1, d), sc, sh, w]
    if glu:
        in_specs.append(pl.BlockSpec((d, tn), lambda i, j: (0, j + nj)))
        args.append(w)
    if bias is not None:
        b2 = bias.reshape(1, -1)
        in_specs.append(pl.BlockSpec((1, tn), lambda i, j: (0, j)))
        args.append(b2)
        if glu:
            in_specs.append(pl.BlockSpec((1, tn), lambda i, j: (0, j + nj)))
            args.append(b2)
    return pl.pallas_call(
        functools.partial(_nmm_kernel, glu=glu, has_bias=bias is not None),
        out_shape=jax.ShapeDtypeStruct((t, n_out), out_dtype),
        grid=(t // tm, nj),
        in_specs=in_specs,
        out_specs=pl.BlockSpec((tm, tn), lambda i, j: (i, j)),
        scratch_shapes=[pltpu.VMEM((tm, d), BF16)],
        compiler_params=_params("arbitrary", "arbitrary"),
        name="norm_mod_matmul",
    )(*args)


def _group_mean_square(xh):
    r = lax.broadcasted_iota(I32, (V_DIM, V_DIM), 0) // HEAD_DIM
    c = lax.broadcasted_iota(I32, (V_DIM, V_DIM), 1) // HEAD_DIM
    ones_blk = jnp.where(r == c, 1.0, 0.0).astype(BF16)
    x2 = xh * xh
    hi = x2.astype(BF16)
    lo = (x2 - hi.astype(F32)).astype(BF16)
    return (_dot(hi, ones_blk) + _dot(lo, ones_blk)) * (1.0 / HEAD_DIM)


def _qk_post_kernel(q_ref, k_ref, v_ref, cos_ref, s1_ref, s2_ref, qg_ref, kg_ref,
                    krow_ref, vrow_ref, qb_ref, kb_ref, vb_ref, *, n_heads):
    cos, s1, s2 = cos_ref[...], s1_ref[...], s2_ref[...]

    def norm_rope(xh, g):
        y = xh * lax.rsqrt(_group_mean_square(xh) + EPS) * g
        return y * cos + pltpu.roll(y, V_DIM - ROPE_DIM // 2, 1) * s1 + pltpu.roll(y, ROPE_DIM // 2, 1) * s2

    for h in range(n_heads):
        sl = slice(h * V_DIM, (h + 1) * V_DIM)
        qh = norm_rope(q_ref[:, sl], qg_ref[...])
        kh = norm_rope(k_ref[:, sl], kg_ref[...])
        krow_ref[:, sl] = kh
        kb_ref[:, sl] = kh.astype(BF16)
        qb_ref[:, sl] = (qh * Q_SCALE).astype(BF16)
    v = v_ref[...]
    vrow_ref[...] = v
    vb_ref[...] = v.astype(BF16)


def _rope_tables(pos):
    half = ROPE_DIM // 2
    inv_freq = jnp.exp(-math.log(ROPE_THETA) * jnp.arange(half, dtype=F32) / half)
    ang = pos.astype(F32)[:, None] * inv_freq[None, :]
    cos, sin = jnp.cos(ang), jnp.sin(ang)
    p = pos.shape[0]
    ones = jnp.ones((p, HEAD_DIM - ROPE_DIM), F32)
    zeros = jnp.zeros((p, HEAD_DIM - ROPE_DIM), F32)
    z8 = jnp.zeros((p, half), F32)
    c64 = jnp.concatenate([cos, cos, ones], axis=1)
    s1_64 = jnp.concatenate([-sin, z8, zeros], axis=1)
    s2_64 = jnp.concatenate([z8, sin, zeros], axis=1)
    two = lambda a: jnp.concatenate([a, a], axis=1)
    return two(c64), two(s1_64), two(s2_64)


def _qk_post(z, tables, q_g, k_g, *, tm, d):
    t = z.shape[0]
    n_heads = d // V_DIM
    p = tables[0].shape[0]
    npb = p // tm
    row = lambda c: pl.BlockSpec((tm, d), lambda i: (i, c))
    tab = pl.BlockSpec((tm, V_DIM), lambda i: (i % npb, 0))
    gspec = pl.BlockSpec((1, V_DIM), lambda i: (0, 0))
    out = pl.BlockSpec((tm, d), lambda i: (i, 0))
    g2 = lambda g: jnp.concatenate([g, g]).reshape(1, V_DIM)
    return pl.pallas_call(
        functools.partial(_qk_post_kernel, n_heads=n_heads),
        out_shape=[jax.ShapeDtypeStruct((t, d), F32), jax.ShapeDtypeStruct((t, d), F32),
                   jax.ShapeDtypeStruct((t, d), BF16), jax.ShapeDtypeStruct((t, d), BF16),
                   jax.ShapeDtypeStruct((t, d), BF16)],
        grid=(t // tm,),
        in_specs=[row(0), row(1), row(2), tab, tab, tab, gspec, gspec],
        out_specs=[out, out, out, out, out],
        compiler_params=_params("arbitrary"),
        name="qk_post",
    )(z, z, z, *tables, g2(q_g), g2(k_g))


def _diff_lambda(lq1, lk1, lq2, lk2, lam_init):
    a = jnp.sum(lq1 * lk1, axis=-1, keepdims=True)
    b = jnp.sum(lq2 * lk2, axis=-1, keepdims=True)
    return jnp.exp(a) - jnp.exp(b) + lam_init


def _subln(o, g, lam_init):
    ms = jnp.mean(o * o, axis=-1, keepdims=True)
    return o * lax.rsqrt(ms + EPS) * g * (1.0 - lam_init)


def _flash_kernel(q_ref, k_ref, v_ref, lq1, lk1, lq2, lk2, sg_ref, o_ref,
                  qm_scr, m_scr, acc_scr, *, lam_init, tq, hb):
    qi, ki, nk = pl.program_id(2), pl.program_id(3), pl.num_programs(3)
    nchunk = tq // V_DIM

    @pl.when(ki == 0)
    def _():
        for h in range(hb):
            q = q_ref[:, h * V_DIM:(h + 1) * V_DIM]
            lane = lax.broadcasted_iota(I32, q.shape, 1)
            zero = jnp.zeros_like(q)
            qm_scr[h, pl.ds(0, tq), :] = jnp.where(lane < HEAD_DIM, q, zero)
            qm_scr[h, pl.ds(tq, tq), :] = jnp.where(lane >= HEAD_DIM, q, zero)
        m_scr[...] = jnp.full(m_scr.shape, -jnp.inf, F32)
        acc_scr[...] = jnp.zeros(acc_scr.shape, F32)

    def tile_update(on_diagonal):
        for h in range(hb):
            k = k_ref[:, h * V_DIM:(h + 1) * V_DIM]
            v = v_ref[:, h * V_DIM:(h + 1) * V_DIM]
            v_ext = jnp.concatenate([v, jnp.ones_like(v)], axis=1)
            s = _dot_nt(qm_scr[h], k)
            if on_diagonal:
                row = lax.broadcasted_iota(I32, s.shape, 0) % tq
                col = lax.broadcasted_iota(I32, s.shape, 1)
                s = jnp.where(col <= row, s, -jnp.inf)
            m_old = m_scr[h]
            m_new = jnp.maximum(m_old, jnp.max(s, axis=-1, keepdims=True))
            alpha = jnp.exp2(m_old - m_new)
            p = jnp.concatenate([jnp.exp2(s[:, j * V_DIM:(j + 1) * V_DIM] - m_new) for j in range(nchunk)], axis=1)
            acc_scr[h] = jnp.concatenate([alpha, alpha], axis=1) * acc_scr[h] + _dot(p.astype(BF16), v_ext)
            m_scr[h] = m_new

    @pl.when(ki < qi)
    def _():
        tile_update(False)

    @pl.when(ki == qi)
    def _():
        tile_update(True)

    @pl.when(ki == nk - 1)
    def _():
        lam = _diff_lambda(lq1[...], lk1[...], lq2[...], lk2[...], lam_init)
        for h in range(hb):
            acc = acc_scr[h]
            o = acc[:tq, :V_DIM] / acc[:tq, V_DIM:] - lam * (acc[tq:, :V_DIM] / acc[tq:, V_DIM:])
            o_ref[:, h * V_DIM:(h + 1) * V_DIM] = _subln(o, sg_ref[...], lam_init).astype(o_ref.dtype)


def _flash_prompt(qb, kb, vb, lams, subln_g, *, b, s_len, d, lam_init, tq, hb):
    n_heads = d // V_DIM
    nq = s_len // tq
    w = hb * V_DIM
    lam_spec = pl.BlockSpec((1, HEAD_DIM), lambda bi, h, qi, ki: (0, 0))
    return pl.pallas_call(
        functools.partial(_flash_kernel, lam_init=lam_init, tq=tq, hb=hb),
        out_shape=jax.ShapeDtypeStruct((b * s_len, d), BF16),
        grid=(b, n_heads // hb, nq, nq),
        in_specs=[pl.BlockSpec((tq, w), lambda bi, h, qi, ki: (bi * nq + qi, h)),
                  pl.BlockSpec((tq, w), lambda bi, h, qi, ki: (bi * nq + jnp.minimum(ki, qi), h)),
                  pl.BlockSpec((tq, w), lambda bi, h, qi, ki: (bi * nq + jnp.minimum(ki, qi), h)),
                  lam_spec, lam_spec, lam_spec, lam_spec,
                  pl.BlockSpec((1, V_DIM), lambda bi, h, qi, ki: (0, 0))],
        out_specs=pl.BlockSpec((tq, w), lambda bi, h, qi, ki: (bi * nq + qi, h)),
        scratch_shapes=[pltpu.VMEM((hb, 2 * tq, V_DIM), BF16), pltpu.VMEM((hb, 2 * tq, V_DIM), F32),
                        pltpu.VMEM((hb, 2 * tq, 2 * V_DIM), F32)],
        compiler_params=_params("arbitrary", "arbitrary", "arbitrary", "arbitrary"),
        name="flash_prompt",
    )(qb, kb, vb, *[x.reshape(1, HEAD_DIM) for x in lams], subln_g.reshape(1, V_DIM))


def _decode_kernel(pt_ref, q_ref, kn_ref, vn_ref, lq1, lk1, lq2, lk2, sg_ref, *refs,
                   lam_init, n_heads, pages, t_new, page_base):
    ck_hbm, cv_hbm, o_ref, kbuf, vbuf, sems, q_scr, m_scr, l_scr, acc_scr = refs
    bi, ci, nc = pl.program_id(0), pl.program_id(1), pl.num_programs(1)
    step, n_steps = bi * nc + ci, pl.num_programs(0) * nc
    rows = 2 * t_new * n_heads
    rows_per_page = PAGE_SIZE * n_heads
    row_head = lax.broadcasted_iota(I32, (rows, V_DIM), 0) // (2 * t_new)
    lane = lax.broadcasted_iota(I32, (rows, V_DIM), 1)

    def page_copies(g, slot):
        b, c = g // nc, g % nc
        copies = []
        for p in range(pages):
            page = page_base + pt_ref[b, c * pages + p]
            dst = pl.ds(p * rows_per_page, rows_per_page)
            copies.append(pltpu.make_async_copy(ck_hbm.at[page], kbuf.at[slot, dst], sems.at[slot, 0]))
            copies.append(pltpu.make_async_copy(cv_hbm.at[page], vbuf.at[slot, dst], sems.at[slot, 1]))
        return copies

    slot = step % 2

    @pl.when(step == 0)
    def _():
        for cp in page_copies(step, slot):
            cp.start()

    @pl.when(step + 1 < n_steps)
    def _():
        for cp in page_copies(step + 1, 1 - slot):
            cp.start()

    @pl.when(ci == 0)
    def _():
        comp = (lax.broadcasted_iota(I32, (rows, V_DIM), 0) // t_new) % 2
        q = q_ref[0]
        q_scr[...] = jnp.where(lane // HEAD_DIM == comp, q, jnp.zeros_like(q))
        n_new = t_new * n_heads
        s = _dot_nt(q_scr[...], kn_ref[0].astype(BF16))
        r = lax.broadcasted_iota(I32, s.shape, 0)
        c = lax.broadcasted_iota(I32, s.shape, 1)
        visible = (c % n_heads == r // (2 * t_new)) & (c // n_heads <= r % t_new)
        s = jnp.where(visible, s, -jnp.inf)
        m = jnp.max(s, axis=-1, keepdims=True)
        p = jnp.exp2(s - m)
        m_scr[...] = jnp.broadcast_to(m, m_scr.shape)
        l_scr[...] = jnp.broadcast_to(jnp.sum(p, axis=-1, keepdims=True), l_scr.shape)
        acc_scr[...] = _dot(p.astype(BF16), vn_ref[0].astype(BF16))

    for cp in page_copies(step, slot):
        cp.wait()
    k2 = kbuf[slot].astype(BF16)
    v2 = vbuf[slot].astype(BF16)
    s = _dot_nt(q_scr[...], k2)
    bias = jnp.where(lane % n_heads == row_head, 0.0, -jnp.inf)
    chunks = [s[:, j * V_DIM:(j + 1) * V_DIM] + bias for j in range(s.shape[1] // V_DIM)]
    m_old = m_scr[...]
    m_new = jnp.maximum(m_old, jnp.max(functools.reduce(jnp.maximum, chunks), axis=-1, keepdims=True))
    alpha = jnp.exp2(m_old - m_new)
    ps = [jnp.exp2(ch - m_new) for ch in chunks]
    l_scr[...] = alpha * l_scr[...] + jnp.sum(functools.reduce(jnp.add, ps), axis=-1, keepdims=True)
    acc_scr[...] = alpha * acc_scr[...] + _dot(jnp.concatenate(ps, axis=1).astype(BF16), v2)
    m_scr[...] = m_new

    @pl.when(ci == nc - 1)
    def _():
        lam = _diff_lambda(lq1[...], lk1[...], lq2[...], lk2[...], lam_init)
        o = acc_scr[...] / l_scr[...]
        o = o - lam * pltpu.roll(o, rows - t_new, 0)
        o_ref[0] = _subln(o, sg_ref[...], lam_init)


def _decode_attn(qb, k_new, v_new, cache_k, cache_v, li, page_table, lams, subln_g, *, bd, t_new, lam_init, pages):
    d = qb.shape[1]
    n_heads = d // V_DIM
    n_pool = cache_k.shape[1]
    rows_per_page = PAGE_SIZE * n_heads
    ck = cache_k.reshape(-1, rows_per_page, V_DIM)
    cv = cache_v.reshape(-1, rows_per_page, V_DIM)
    n_pages = page_table.shape[1]
    nc = n_pages // pages
    rows = 2 * t_new * n_heads
    n_new = t_new * n_heads
    q2 = jnp.broadcast_to(qb.reshape(bd, t_new, n_heads, 1, V_DIM).transpose(0, 2, 3, 1, 4),
                          (bd, n_heads, 2, t_new, V_DIM)).reshape(bd, rows, V_DIM)

    new_spec = pl.BlockSpec((1, n_new, V_DIM), lambda b, c, pt: (b, 0, 0))
    lam_spec = pl.BlockSpec((1, HEAD_DIM), lambda b, c, pt: (0, 0))
    chunk_rows = pages * rows_per_page
    grid_spec = pltpu.PrefetchScalarGridSpec(
        num_scalar_prefetch=1,
        grid=(bd, nc),
        in_specs=[pl.BlockSpec((1, rows, V_DIM), lambda b, c, pt: (b, 0, 0)), new_spec, new_spec,
                  lam_spec, lam_spec, lam_spec, lam_spec, pl.BlockSpec((1, V_DIM), lambda b, c, pt: (0, 0)),
                  pl.BlockSpec(memory_space=pl.ANY), pl.BlockSpec(memory_space=pl.ANY)],
        out_specs=pl.BlockSpec((1, rows, V_DIM), lambda b, c, pt: (b, 0, 0)),
        scratch_shapes=[pltpu.VMEM((2, chunk_rows, V_DIM), F32), pltpu.VMEM((2, chunk_rows, V_DIM), F32),
                        pltpu.SemaphoreType.DMA((2, 2)),
                        pltpu.VMEM((rows, V_DIM), BF16), pltpu.VMEM((rows, V_DIM), F32),
                        pltpu.VMEM((rows, V_DIM), F32), pltpu.VMEM((rows, V_DIM), F32)])
    o2 = pl.pallas_call(
        functools.partial(_decode_kernel, lam_init=lam_init, n_heads=n_heads, pages=pages, t_new=t_new,
                          page_base=li * n_pool),
        out_shape=jax.ShapeDtypeStruct((bd, rows, V_DIM), F32),
        grid_spec=grid_spec,
        compiler_params=_params("arbitrary", "arbitrary"),
        name="decode_attn",
    )(page_table, q2, k_new.reshape(bd, n_new, V_DIM), v_new.reshape(bd, n_new, V_DIM),
      *[x.reshape(1, HEAD_DIM) for x in lams], subln_g.reshape(1, V_DIM), ck, cv)
    o = o2.reshape(bd, n_heads, 2, t_new, V_DIM)[:, :, 0].transpose(0, 2, 1, 3)
    return o.reshape(bd * t_new, d).astype(BF16)


def _layernorm_silu(y, g, b):
    mu = jnp.mean(y, axis=-1, keepdims=True)
    yc = y - mu
    var = jnp.mean(yc * yc, axis=-1, keepdims=True)
    z = yc * lax.rsqrt(var + EPS) * g + b
    return z * _sigmoid(z)


CONV_ROW_CHUNK = 128
CONV_LANE_CHUNK = 128


def _depthwise_taps(ext, w_ref, y_scr, *, taps, base, tt):
    c = y_scr.shape[1]
    rc, cc = CONV_ROW_CHUNK, min(CONV_LANE_CHUNK, c)
    by_residue = {}
    for j in range(taps):
        by_residue.setdefault((base + j) % SUBLANES, []).append(j)

    def row_chunk(i, carry):
        row0 = pl.multiple_of(i * rc, rc)
        for c0 in range(0, c, cc):
            y = None
            for r, js in sorted(by_residue.items()):
                acc = None
                for j in js:
                    aligned = base + j - r
                    term = (w_ref[pl.ds(j, 1), pl.ds(c0, cc)]
                            * ext[pl.ds(row0 + aligned, rc + SUBLANES), pl.ds(c0, cc)])
                    acc = term if acc is None else acc + term
                part = acc[r:r + rc]
                y = part if y is None else y + part
            y_scr[pl.ds(row0, rc), pl.ds(c0, cc)] = y
        return carry

    lax.fori_loop(0, tt // rc, row_chunk, 0)


def _conv_prompt_kernel(*refs, taps, halo, tt, gated, ln):
    refs = list(refs)
    u_ref, uh_ref, prev_ref = refs[:3]
    del refs[:3]
    if gated:
        gc_ref, gch_ref, gb_ref = refs[:3]
        del refs[:3]
    w_ref = refs.pop(0)
    if ln:
        cb_ref, lg_ref, lb_ref = refs[:3]
        del refs[:3]
    o_ref = refs.pop(0)
    if gated:
        tail_ref = refs.pop(0)
    ext, y_scr = refs
    ti, nt = pl.program_id(1), pl.num_programs(1)

    cur = u_ref[...]
    if gated:
        cur = cur * gc_ref[...]
    ext[pl.ds(halo, tt), :] = cur

    @pl.when(ti == 0)
    def _():
        ext[pl.ds(0, halo), :] = prev_ref[0]

    @pl.when(ti > 0)
    def _():
        hv = uh_ref[...]
        if gated:
            hv = hv * gch_ref[...]
        ext[pl.ds(0, halo), :] = hv

    ext[pl.ds(halo + tt, SUBLANES), :] = jnp.zeros((SUBLANES, ext.shape[1]), F32)
    _depthwise_taps(ext, w_ref, y_scr, taps=taps, base=halo - (taps - 1), tt=tt)
    y = y_scr[...]
    if ln:
        y = _layernorm_silu(y + cb_ref[...], lg_ref[...], lb_ref[...])
    if gated:
        y = gb_ref[...] * y

        @pl.when(ti == nt - 1)
        def _():
            tail_ref[0] = ext[pl.ds(halo + tt - SUBLANES, SUBLANES), :]
    o_ref[...] = y.astype(o_ref.dtype)


def _conv_prompt(src, ucol, prev, w, *, b, s_len, c, tt, gate_cols=None, ln_params=None):
    taps = w.shape[0]
    halo = SUBLANES if taps - 1 <= SUBLANES else 32
    nt = s_len // tt
    hpt = tt // halo
    prev_p = jnp.pad(prev, ((0, 0), (halo - (taps - 1), 0), (0, 0)))
    cur = lambda col: pl.BlockSpec((tt, c), lambda bi, ti: (bi * nt + ti, col))
    hal = lambda col: pl.BlockSpec((halo, c), lambda bi, ti: (jnp.maximum((bi * nt + ti) * hpt - 1, 0), col))
    vec = pl.BlockSpec((1, c), lambda bi, ti: (0, 0))
    in_specs = [cur(ucol), hal(ucol), pl.BlockSpec((1, halo, c), lambda bi, ti: (bi, 0, 0))]
    args = [src, src, prev_p]
    gated = gate_cols is not None
    if gated:
        cc, cb = gate_cols
        in_specs += [cur(cc), hal(cc), cur(cb)]
        args += [src, src, src]
    in_specs.append(pl.BlockSpec((taps, c), lambda bi, ti: (0, 0)))
    args.append(w)
    if ln_params is not None:
        in_specs += [vec, vec, vec]
        args += [p.reshape(1, c) for p in ln_params]
    out_shape = [jax.ShapeDtypeStruct((b * s_len, c), BF16)]
    out_specs = [pl.BlockSpec((tt, c), lambda bi, ti: (bi * nt + ti, 0))]
    if gated:
        out_shape.append(jax.ShapeDtypeStruct((b, SUBLANES, c), F32))
        out_specs.append(pl.BlockSpec((1, SUBLANES, c), lambda bi, ti: (bi, 0, 0)))
    res = pl.pallas_call(
        functools.partial(_conv_prompt_kernel, taps=taps, halo=halo, tt=tt, gated=gated, ln=ln_params is not None),
        out_shape=out_shape,
        grid=(b, nt),
        in_specs=in_specs,
        out_specs=out_specs,
        scratch_shapes=[pltpu.VMEM((halo + tt + SUBLANES, c), F32), pltpu.VMEM((tt, c), F32)],
        compiler_params=_params("arbitrary", "arbitrary"),
        name="conv_prompt",
    )(*args)
    return res if gated else res[0]


def _conv_sample_kernel(*refs, taps, s_new, gated, ln):
    refs = list(refs)
    prev_ref, u_ref = refs[:2]
    del refs[:2]
    if gated:
        gc_ref, gb_ref = refs[:2]
        del refs[:2]
    w_ref = refs.pop(0)
    if ln:
        cb_ref, lg_ref, lb_ref = refs[:3]
        del refs[:3]
    o_ref, cu_ref, ext = refs
    for j in range(taps - 1):
        ext[j] = prev_ref[j]
    for s in range(s_new):
        cu = u_ref[s]
        if gated:
            cu = cu * gc_ref[s]
        cu_ref[s] = cu
        ext[taps - 1 + s] = cu
    for s in range(s_new):
        y = w_ref[0] * ext[s]
        for j in range(1, taps):
            y = y + w_ref[j] * ext[s + j]
        if ln:
            y = _layernorm_silu(y + cb_ref[...], lg_ref[...], lb_ref[...])
        if gated:
            y = gb_ref[s] * y
        o_ref[s] = y.astype(o_ref.dtype)


def _conv_sample(u, prev, w, *, gates=None, ln_params=None):
    s_new, bd, c = u.shape
    taps = w.shape[0]
    args = [prev, u]
    gated = gates is not None
    if gated:
        args += list(gates)
    args.append(w.reshape(taps, 1, c))
    if ln_params is not None:
        args += [p.reshape(1, c) for p in ln_params]
    return pl.pallas_call(
        functools.partial(_conv_sample_kernel, taps=taps, s_new=s_new, gated=gated, ln=ln_params is not None),
        out_shape=[jax.ShapeDtypeStruct((s_new, bd, c), BF16), jax.ShapeDtypeStruct((s_new, bd, c), F32)],
        scratch_shapes=[pltpu.VMEM((taps - 1 + s_new, bd, c), F32)],
        compiler_params=pltpu.CompilerParams(vmem_limit_bytes=VMEM_LIMIT_BYTES),
        name="conv_sample",
    )(*args)


def _mm_res_kernel(*refs, n_in, has_bias):
    a_refs, w_refs = refs[:n_in], refs[n_in:2 * n_in]
    rest = list(refs[2 * n_in:])
    b_ref = rest.pop(0) if has_bias else None
    x_ref, gate_ref, o_ref = rest
    acc = _dot(a_refs[0][...], w_refs[0][...])
    for a_ref, w_ref in zip(a_refs[1:], w_refs[1:]):
        acc = acc + _dot(a_ref[...], w_ref[...])
    if has_bias:
        acc = acc + b_ref[...]
    o_ref[...] = x_ref[...] + gate_ref[0] * acc


def _matmul_residual(a_list, w_list, bias, x, gate, *, tm, tn):
    t, d = x.shape
    ns, r, _ = gate.shape
    tps = (t // tm) // ns
    n_in = len(a_list)
    in_specs = [pl.BlockSpec((tm, a.shape[1]), lambda i, j: (i, 0)) for a in a_list]
    in_specs += [pl.BlockSpec((w.shape[0], tn), lambda i, j: (0, j)) for w in w_list]
    args = list(a_list) + list(w_list)
    if bias is not None:
        in_specs.append(pl.BlockSpec((1, tn), lambda i, j: (0, j)))
        args.append(bias.reshape(1, d))
    in_specs += [pl.BlockSpec((tm, tn), lambda i, j: (i, j)),
                 pl.BlockSpec((1, r, tn), lambda i, j: (i // tps, 0, j))]
    args += [x, gate]
    return pl.pallas_call(
        functools.partial(_mm_res_kernel, n_in=n_in, has_bias=bias is not None),
        out_shape=jax.ShapeDtypeStruct((t, d), F32),
        grid=(t // tm, d // tn),
        in_specs=in_specs,
        out_specs=pl.BlockSpec((tm, tn), lambda i, j: (i, j)),
        compiler_params=_params("arbitrary", "arbitrary"),
        name="matmul_residual",
    )(*args)


def _router_kernel(x_ref, g_ref, sc_ref, sh_ref, rw_ref, rb_ref,
                   hb_ref, tw_ref, pos_ref, sizes_ref, *, tm):
    h = _rmsnorm_mod(x_ref[...], g_ref[...], sc_ref[0], sh_ref[0])
    hb_ref[...] = h.astype(BF16)
    logits = lax.dot_general(rw_ref[...], h, (((1,), (1,)), ((), ())),
                             precision=lax.Precision.HIGHEST, preferred_element_type=F32) + rb_ref[...]
    e_iota = lax.broadcasted_iota(I32, logits.shape, 0)
    work = logits
    ids, vals = [], []
    multi_hot = jnp.zeros(logits.shape, F32)
    for _ in range(TOP_K):
        m = jnp.max(work, axis=0, keepdims=True)
        idx = jnp.min(jnp.where(work == m, e_iota, N_EXPERTS), axis=0, keepdims=True)
        sel = e_iota == idx
        work = jnp.where(sel, -jnp.inf, work)
        multi_hot = multi_hot + jnp.where(sel, 1.0, 0.0)
        ids.append(idx)
        vals.append(m)
    ex = [jnp.exp(v - vals[0]) for v in vals]
    den = ex[0] + ex[1] + ex[2] + ex[3]
    r = lax.broadcasted_iota(I32, (tm, tm), 0)
    c = lax.broadcasted_iota(I32, (tm, tm), 1)
    before = jnp.where(r < c, 1.0, 0.0).astype(BF16)
    earlier = _dot(multi_hot.astype(BF16), before)
    cnt = jnp.sum(multi_hot, axis=1, keepdims=True)
    sizes = jnp.ceil(cnt * (1.0 / MOE_ROW_ALIGN)) * MOE_ROW_ALIGN
    er = lax.broadcasted_iota(I32, (N_EXPERTS, N_EXPERTS), 0)
    ec = lax.broadcasted_iota(I32, (N_EXPERTS, N_EXPERTS), 1)
    lower = jnp.where(ec < er, 1.0, 0.0).astype(BF16)
    run_start = _dot(lower, jnp.broadcast_to(sizes, (N_EXPERTS, V_DIM)).astype(BF16))[:, :1]
    row_in_tile = run_start + earlier
    for k in range(TOP_K):
        tw_ref[pl.ds(k, 1), :] = ex[k] / den
        pos_ref[pl.ds(k, 1), :] = jnp.sum(jnp.where(e_iota == ids[k], row_in_tile, 0.0),
                                          axis=0, keepdims=True).astype(I32)
    sizes_ref[0] = sizes.astype(I32)


def _router(x, g, sc, sh, rw_t, rb, *, tm):
    t, d = x.shape
    ns, r, _ = sc.shape
    nt = t // tm
    tps = nt // ns
    mod = pl.BlockSpec((1, r if r == 1 else tm, d), lambda i: (i // tps, 0 if r == 1 else i % tps, 0))
    lane4 = pl.BlockSpec((TOP_K, tm), lambda i: (0, i))
    return pl.pallas_call(
        functools.partial(_router_kernel, tm=tm),
        out_shape=[jax.ShapeDtypeStruct((t, d), BF16), jax.ShapeDtypeStruct((TOP_K, t), F32),
                   jax.ShapeDtypeStruct((TOP_K, t), I32), jax.ShapeDtypeStruct((nt, N_EXPERTS, 1), I32)],
        grid=(nt,),
        in_specs=[pl.BlockSpec((tm, d), lambda i: (i, 0)), pl.BlockSpec((1, d), lambda i: (0, 0)), mod, mod,
                  pl.BlockSpec((N_EXPERTS, d), lambda i: (0, 0)), pl.BlockSpec((N_EXPERTS, 1), lambda i: (0, 0))],
        out_specs=[pl.BlockSpec((tm, d), lambda i: (i, 0)), lane4, lane4,
                   pl.BlockSpec((1, N_EXPERTS, 1), lambda i: (i, 0, 0))],
        compiler_params=_params("arbitrary"),
        name="moe_router",
    )(x, g.reshape(1, d), sc, sh, rw_t, rb.reshape(N_EXPERTS, 1))


def _for_each_run(sizes_ref, offs_ref, tile, action):
    def one(e, local):
        n = pl.multiple_of(sizes_ref[tile * N_EXPERTS + e], MOE_ROW_ALIGN)
        off = pl.multiple_of(offs_ref[tile * N_EXPERTS + e], MOE_ROW_ALIGN)
        loc = pl.multiple_of(local, MOE_ROW_ALIGN)

        @pl.when(n > 0)
        def _():
            action(loc, off, n)
        return local + n

    lax.fori_loop(0, N_EXPERTS, one, 0)


def _zero_fill_unused(gap_ref, nu_ref, xb_hbm, zero_scr, sem, *, bm, nb):
    zero_scr[...] = jnp.zeros(zero_scr.shape, F32)

    def gap_copy(e):
        n = pl.multiple_of(gap_ref[N_EXPERTS + e], MOE_ROW_ALIGN)
        off = pl.multiple_of(gap_ref[e], MOE_ROW_ALIGN)
        return n, pltpu.make_async_copy(zero_scr.at[pl.ds(0, n)], xb_hbm.at[pl.ds(off, n)], sem)

    def blk_copy(b):
        return pltpu.make_async_copy(zero_scr, xb_hbm.at[pl.ds(pl.multiple_of(b * bm, bm), bm)], sem)

    def each_gap(action):
        def body(e, carry):
            n, cp = gap_copy(e)

            @pl.when(n > 0)
            def _():
                action(cp)
            return carry
        lax.fori_loop(0, N_EXPERTS, body, 0)

    def each_blk(action):
        def body(b, carry):
            action(blk_copy(b))
            return carry
        lax.fori_loop(nu_ref[0], nb, body, 0)

    each_gap(lambda cp: cp.start())
    each_blk(lambda cp: cp.start())
    each_gap(lambda cp: cp.wait())
    each_blk(lambda cp: cp.wait())


def _moe_rows(tm):
    return tm * TOP_K + N_EXPERTS * MOE_ROW_ALIGN


def _dispatch_kernel(sizes_ref, offs_ref, gap_ref, nu_ref, *refs, tile_ranges, bm, nb):
    n_groups = len(tile_ranges)
    hb_refs, pos_refs = refs[0:2 * n_groups:2], refs[1:2 * n_groups:2]
    xb_hbm, sorted_scr, zero_scr, sems = refs[2 * n_groups:]
    tile, n_tiles = pl.program_id(0), pl.num_programs(0)
    slot = tile % 2

    for hb_ref, pos_ref, (first, last) in zip(hb_refs, pos_refs, tile_ranges):
        @pl.when((tile >= first) & (tile < last))
        def _():
            tm = hb_ref.shape[0]
            rows = _moe_rows(tm)
            r_iota = lax.broadcasted_iota(I32, (rows, tm), 0)
            hit = r_iota == pos_ref[pl.ds(0, 1), :]
            for k in range(1, TOP_K):
                hit = hit | (r_iota == pos_ref[pl.ds(k, 1), :])
            perm = jnp.where(hit, 1.0, 0.0).astype(BF16)
            sorted_scr[slot, pl.ds(0, rows), :] = _dot(perm, hb_ref[...])

    def run_copy(s):
        def make(loc, off, n):
            return pltpu.make_async_copy(sorted_scr.at[s, pl.ds(loc, n)], xb_hbm.at[pl.ds(off, n)], sems.at[s])
        return make

    _for_each_run(sizes_ref, offs_ref, tile, lambda *a: run_copy(slot)(*a).start())

    @pl.when(tile > 0)
    def _():
        _for_each_run(sizes_ref, offs_ref, tile - 1, lambda *a: run_copy(1 - slot)(*a).wait())

    @pl.when(tile == n_tiles - 1)
    def _():
        _for_each_run(sizes_ref, offs_ref, tile, lambda *a: run_copy(slot)(*a).wait())
        _zero_fill_unused(gap_ref, nu_ref, xb_hbm, zero_scr, sems.at[0], bm=bm, nb=nb)


def _dispatch(groups, sizes, offs, gaps, n_used, *, cap, bm):
    d = groups[0][0].shape[1]
    in_specs, args, tile_ranges = [], [], []
    first = 0
    for hb, pos_t, tm in groups:
        nt = hb.shape[0] // tm
        local = lambda i, first=first, nt=nt: jnp.clip(i - first, 0, nt - 1)
        in_specs += [pl.BlockSpec((tm, d), lambda i, *_, local=local: (local(i), 0)),
                     pl.BlockSpec((TOP_K, tm), lambda i, *_, local=local: (0, local(i)))]
        args += [hb, pos_t]
        tile_ranges.append((first, first + nt))
        first += nt
    rows = max(_moe_rows(tm) for _, _, tm in groups)
    grid_spec = pltpu.PrefetchScalarGridSpec(
        num_scalar_prefetch=4,
        grid=(first,),
        in_specs=in_specs,
        out_specs=pl.BlockSpec(memory_space=pl.ANY),
        scratch_shapes=[pltpu.VMEM((2, rows, d), F32), pltpu.VMEM((bm, d), F32), pltpu.SemaphoreType.DMA((2,))])
    return pl.pallas_call(
        functools.partial(_dispatch_kernel, tile_ranges=tuple(tile_ranges), bm=bm, nb=cap // bm),
        out_shape=jax.ShapeDtypeStruct((cap, d), F32),
        grid_spec=grid_spec,
        compiler_params=_params("arbitrary"),
        name="moe_dispatch",
    )(sizes, offs, gaps, n_used, *args)


def _expert_kernel(be_ref, bv_ref, nu_ref, x_ref, wgu_ref, bgu_ref, wdn_ref, bdn_ref, y_ref,
                   wgu_scr, wdn_scr, *, d_ff, sub):
    b = pl.program_id(0)
    prev_e = be_ref[jnp.maximum(b - 1, 0)]

    @pl.when((b == 0) | (be_ref[b] != prev_e))
    def _():
        wgu_scr[...] = wgu_ref[0, 0].astype(BF16)
        wdn_scr[...] = wdn_ref[0, 0].astype(BF16)

    valid = bv_ref[b]
    for start in range(0, x_ref.shape[0], sub):
        rows = pl.ds(start, sub)

        @pl.when(valid > start)
        def _():
            x = x_ref[rows, :]
            row = lax.broadcasted_iota(I32, x.shape, 0) + start
            x = jnp.where(row < valid, x, 0.0).astype(BF16)
            hg = _dot(x, wgu_scr[...]) + bgu_ref[0, 0]
            g = jnp.minimum(hg[:, :d_ff], SWIGLU_LIMIT)
            lin = jnp.clip(hg[:, d_ff:], -SWIGLU_LIMIT, SWIGLU_LIMIT)
            act = g * _sigmoid(SWIGLU_ALPHA * g) * (lin + 1.0)
            y_ref[rows, :] = _dot(act.astype(BF16), wdn_scr[...]) + bdn_ref[0, 0]

        @pl.when(valid <= start)
        def _():
            y_ref[rows, :] = jnp.zeros((sub, y_ref.shape[1]), F32)


def _experts(xb, blk_e, blk_valid, n_used, w_gu, b_gu, w_dn, b_dn, *, layer, bm):
    cap, d = xb.shape
    depth, n_e, _, two_ff = w_gu.shape
    d_ff = two_ff // 2
    nb = cap // bm
    used = lambda b, nu: jnp.minimum(b, nu[0] - 1)
    grid_spec = pltpu.PrefetchScalarGridSpec(
        num_scalar_prefetch=3,
        grid=(nb,),
        in_specs=[pl.BlockSpec((bm, d), lambda b, be, bv, nu: (used(b, nu), 0)),
                  pl.BlockSpec((1, 1, d, two_ff), lambda b, be, bv, nu: (layer, be[b], 0, 0)),
                  pl.BlockSpec((1, 1, 1, two_ff), lambda b, be, bv, nu: (layer, be[b], 0, 0)),
                  pl.BlockSpec((1, 1, d_ff, d), lambda b, be, bv, nu: (layer, be[b], 0, 0)),
                  pl.BlockSpec((1, 1, 1, d), lambda b, be, bv, nu: (layer, be[b], 0, 0))],
        out_specs=pl.BlockSpec((bm, d), lambda b, be, bv, nu: (b, 0)),
        scratch_shapes=[pltpu.VMEM((d, two_ff), BF16), pltpu.VMEM((d_ff, d), BF16)])
    return pl.pallas_call(
        functools.partial(_expert_kernel, d_ff=d_ff, sub=min(MOE_SUB_BLOCK, bm)),
        out_shape=jax.ShapeDtypeStruct((cap, d), F32),
        grid_spec=grid_spec,
        compiler_params=_params("arbitrary"),
        name="moe_experts",
    )(blk_e, blk_valid, n_used, xb, w_gu, b_gu.reshape(depth, n_e, 1, two_ff), w_dn, b_dn.reshape(depth, n_e, 1, d))


def _combine_kernel(sizes_ref, offs_ref, x_ref, gate_ref, pos_ref, tw_ref, yb_hbm, o_ref, ys_scr, sems,
                    *, rows, first_tile):
    tm = x_ref.shape[0]
    step, n_steps = pl.program_id(0), pl.num_programs(0)
    tile = step + first_tile
    slot = step % 2

    def run_copy(s):
        def make(loc, off, n):
            return pltpu.make_async_copy(yb_hbm.at[pl.ds(off, n)], ys_scr.at[s, pl.ds(loc, n)], sems.at[s])
        return make

    @pl.when(step == 0)
    def _():
        ys_scr[...] = jnp.zeros(ys_scr.shape, F32)
        _for_each_run(sizes_ref, offs_ref, tile, lambda *a: run_copy(slot)(*a).start())

    @pl.when(step + 1 < n_steps)
    def _():
        _for_each_run(sizes_ref, offs_ref, tile + 1, lambda *a: run_copy(1 - slot)(*a).start())

    _for_each_run(sizes_ref, offs_ref, tile, lambda *a: run_copy(slot)(*a).wait())

    r_iota = lax.broadcasted_iota(I32, (tm, rows), 1)
    wmat = jnp.zeros((tm, rows), F32)
    for k in range(TOP_K):
        wmat = wmat + jnp.where(r_iota == pos_ref[:, pl.ds(k, 1)], tw_ref[:, pl.ds(k, 1)], 0.0)
    y = _dot(wmat.astype(BF16), ys_scr[slot].astype(BF16))
    o_ref[...] = x_ref[...] + gate_ref[0] * y


def _combine(x, gate, pos, tw, yb, sizes, offs, *, tm, first_tile):
    t, d = x.shape
    ns, r, _ = gate.shape
    nt = t // tm
    tps = nt // ns
    rows = _moe_rows(tm)
    mod = pl.BlockSpec((1, r if r == 1 else tm, d),
                       lambda i, s, o: (i // tps, 0 if r == 1 else i % tps, 0))
    col4 = pl.BlockSpec((tm, TOP_K), lambda i, s, o: (i, 0))
    grid_spec = pltpu.PrefetchScalarGridSpec(
        num_scalar_prefetch=2,
        grid=(nt,),
        in_specs=[pl.BlockSpec((tm, d), lambda i, s, o: (i, 0)), mod, col4, col4,
                  pl.BlockSpec(memory_space=pl.ANY)],
        out_specs=pl.BlockSpec((tm, d), lambda i, s, o: (i, 0)),
        scratch_shapes=[pltpu.VMEM((2, rows, d), F32), pltpu.SemaphoreType.DMA((2,))])
    return pl.pallas_call(
        functools.partial(_combine_kernel, rows=rows, first_tile=first_tile),
        out_shape=jax.ShapeDtypeStruct((t, d), F32),
        grid_spec=grid_spec,
        compiler_params=_params("arbitrary"),
        name="moe_combine",
    )(sizes, offs, x, gate, pos, tw, yb)


def _round_up(x, m):
    return (x + m - 1) // m * m


def _moe_layer(groups, g, rw_t, rb, w_gu, b_gu, w_dn, b_dn, *, layer, bm):
    routed = [_router(x, g, sc, sh, rw_t, rb, tm=tm) for x, sc, sh, _, tm in groups]
    sizes = jnp.concatenate([r[3].reshape(-1, N_EXPERTS) for r in routed], axis=0)
    nt = sizes.shape[0]
    rows_e = jnp.sum(sizes, axis=0)
    region = _round_up(rows_e, bm)
    region_end = jnp.cumsum(region)
    region_base = region_end - region
    offs = region_base[None, :] + jnp.cumsum(sizes, axis=0) - sizes
    t_all = sum(x.shape[0] for x, *_ in groups)
    cap = _round_up(t_all * TOP_K + nt * N_EXPERTS * (MOE_ROW_ALIGN - 1), bm) + N_EXPERTS * bm
    nb = cap // bm
    blk_start = jnp.arange(nb, dtype=I32) * bm
    n_used = (region_end[-1] // bm).astype(I32).reshape(1)
    blk_e = jnp.sum((region_end[None, :] <= blk_start[:, None]).astype(I32), axis=1)
    last_e = jnp.sum((region_end < region_end[-1]).astype(I32))
    blk_e = jnp.where(blk_start < region_end[-1], blk_e, last_e).astype(I32)
    onehot = blk_e[:, None] == jnp.arange(N_EXPERTS, dtype=I32)[None, :]
    rows_end = jnp.sum(jnp.where(onehot, (region_base + rows_e)[None, :], 0), axis=1)
    blk_valid = jnp.clip(rows_end - blk_start, 0, bm).astype(I32)
    sizes_f = sizes.reshape(-1).astype(I32)
    offs_f = offs.reshape(-1).astype(I32)
    gaps = jnp.concatenate([region_base + rows_e, region - rows_e]).astype(I32)
    xb = _dispatch([(r[0], r[2], grp[4]) for r, grp in zip(routed, groups)], sizes_f, offs_f, gaps, n_used,
                   cap=cap, bm=bm)
    yb = _experts(xb, blk_e, blk_valid, n_used, w_gu, b_gu, w_dn, b_dn, layer=layer, bm=bm)
    outs, first = [], 0
    for (x, _, _, gate, tm), (_, top_w, pos_t, _) in zip(groups, routed):
        outs.append(_combine(x, gate, pos_t.T, top_w.T, yb, sizes_f, offs_f, tm=tm, first_tile=first))
        first += x.shape[0] // tm
    return outs


def _hybrid_mixer(x, l, mod, pos, attend, sc_prev, p, *, b, s_len, prompt):
    t, d = x.shape
    tm = min(1024, s_len) if prompt else t
    sh1, sc1, g1 = mod
    li = l // 2
    lam_init = 0.8 - 0.6 * math.exp(-0.3 * l)
    z = _norm_mod_matmul(x, p["norm_mix_g"][l], sc1, sh1, p["hy_w_in"][li], None, tm=tm, tn=768)
    kr, vr, qb, kb, vb = _qk_post(z, _rope_tables(pos), p["q_norm_g"][li], p["k_norm_g"][li], tm=min(512, t), d=d)
    lams = (p["lambda_q1"][li], p["lambda_k1"][li], p["lambda_q2"][li], p["lambda_k2"][li])
    o = attend(qb, kb, vb, kr, vr, lams, p["subln_g"][li], li, lam_init)
    c_sc = p["sconv_w"].shape[-1]
    bcol = 3 * d // c_sc
    ccol, ucol = bcol + 1, bcol + 2
    if prompt:
        o_sc, tail = _conv_prompt(z, ucol, sc_prev[li], p["sconv_w"][li], b=b, s_len=s_len, c=c_sc,
                                  tt=min(512, s_len), gate_cols=(ccol, bcol))
        sc_new = tail[:, SUBLANES - (SC_KERNEL - 1):]
    else:
        tmaj = lambda col: z[:, col * c_sc:(col + 1) * c_sc].reshape(b, s_len, c_sc).transpose(1, 0, 2)
        o_t, cu_t = _conv_sample(tmaj(ucol), sc_prev[li].transpose(1, 0, 2), p["sconv_w"][li],
                                 gates=(tmaj(ccol), tmaj(bcol)))
        o_sc = o_t.transpose(1, 0, 2).reshape(t, c_sc)
        sc_new = jnp.concatenate([sc_prev[li], cu_t.transpose(1, 0, 2)], axis=1)[:, -(SC_KERNEL - 1):]
    w_out = p["hy_w_out"][li]
    x = _matmul_residual([o, o_sc], [w_out[:d], w_out[d:]], None, x, g1, tm=tm, tn=512)
    n_heads = d // V_DIM
    return x, (kr.reshape(b, s_len, n_heads, V_DIM), vr.reshape(b, s_len, n_heads, V_DIM), sc_new)


def _conformer_mixer(x, l, mod, conf_prev, p, *, b, s_len, prompt):
    t, d = x.shape
    tm = min(1024, s_len) if prompt else t
    sh1, sc1, g1 = mod
    ci = l // 2
    u = _norm_mod_matmul(x, p["norm_mix_g"][l], sc1, sh1, p["conf_w1"][ci], p["conf_b1"][ci], tm=tm, tn=512, glu=True)
    ln_params = (p["conf_dw_b"][ci], p["conf_ln_g"][ci], p["conf_ln_b"][ci])
    if prompt:
        y = _conv_prompt(u, 0, conf_prev[ci], p["conf_dw"][ci], b=b, s_len=s_len, c=d, tt=min(512, s_len),
                         ln_params=ln_params)
        cf_new = u.reshape(b, s_len, d)[:, s_len - (CONF_KERNEL - 1):]
    else:
        u_t = u.reshape(b, s_len, d).transpose(1, 0, 2)
        y_t, _ = _conv_sample(u_t, conf_prev[ci].transpose(1, 0, 2), p["conf_dw"][ci], ln_params=ln_params)
        y = y_t.transpose(1, 0, 2).reshape(t, d)
        cf_new = jnp.concatenate([conf_prev[ci], u.reshape(b, s_len, d)], axis=1)[:, -(CONF_KERNEL - 1):]
    x = _matmul_residual([y], [p["conf_w2"][ci]], p["conf_b2"][ci], x, g1, tm=tm, tn=512)
    return x, (cf_new,)


def _mixer(x, l, mod, pos, attend, sc_prev, conf_prev, p, *, b, s_len, prompt):
    if l % 2 == 0:
        return _hybrid_mixer(x, l, mod, pos, attend, sc_prev, p, b=b, s_len=s_len, prompt=prompt)
    return _conformer_mixer(x, l, mod, conf_prev, p, b=b, s_len=s_len, prompt=prompt)


def kernel(x_prompt, x_sample, cache_k, cache_v, state_sconv, state_conf, page_table, c_prompt, c_sample,
           ada_w, ada_b, norm_mix_g, norm_ffn_g, hy_w_in, hy_w_out, q_norm_g, k_norm_g,
           lambda_q1, lambda_k1, lambda_q2, lambda_k2, subln_g, sconv_w,
           conf_w1, conf_b1, conf_dw, conf_dw_b, conf_ln_g, conf_ln_b, conf_w2, conf_b2,
           router_w, router_b, moe_w_gu, moe_b_gu, moe_w_dn, moe_b_dn):
    bp, s_len, d = x_prompt.shape
    bd, t_new, _ = x_sample.shape
    depth = ada_w.shape[0]
    p = {"ada_w": ada_w, "norm_mix_g": norm_mix_g, "norm_ffn_g": norm_ffn_g,
         "hy_w_in": hy_w_in.astype(BF16), "hy_w_out": hy_w_out.astype(BF16),
         "q_norm_g": q_norm_g, "k_norm_g": k_norm_g,
         "lambda_q1": lambda_q1, "lambda_k1": lambda_k1, "lambda_q2": lambda_q2, "lambda_k2": lambda_k2,
         "subln_g": subln_g, "sconv_w": sconv_w,
         "conf_w1": conf_w1.astype(BF16), "conf_b1": conf_b1, "conf_dw": conf_dw, "conf_dw_b": conf_dw_b,
         "conf_ln_g": conf_ln_g, "conf_ln_b": conf_ln_b, "conf_w2": conf_w2.astype(BF16), "conf_b2": conf_b2,
         "router_w_t": jnp.swapaxes(router_w, 1, 2), "router_b": router_b,
         "moe_w_gu": moe_w_gu, "moe_b_gu": moe_b_gu, "moe_w_dn": moe_w_dn, "moe_b_dn": moe_b_dn}

    mod_all = _ada_mod(jnp.concatenate([c_prompt, c_sample], axis=0), ada_w, ada_b)
    mods_p, mods_s = [], []
    for l in range(depth):
        parts = [mod_all[l, :, i * d:(i + 1) * d] for i in range(6)]
        mods_p.append([m[:bp].reshape(bp, 1, d) for m in parts])
        mods_s.append([jnp.repeat(m[bp:], t_new, axis=0).reshape(1, bd * t_new, d) for m in parts])

    past_len = page_table.shape[1] * PAGE_SIZE
    pos_prompt = jnp.arange(s_len, dtype=I32)
    pos_sample = jnp.tile(past_len + jnp.arange(t_new, dtype=I32), bd)

    def attend_prompt(qb, kb, vb, kr, vr, lams, sg, li, lam_init):
        return _flash_prompt(qb, kb, vb, lams, sg, b=bp, s_len=s_len, d=d, lam_init=lam_init,
                             tq=min(512, s_len), hb=4)

    def attend_sample(qb, kb, vb, kr, vr, lams, sg, li, lam_init):
        return _decode_attn(qb, kr, vr, cache_k, cache_v, li, page_table, lams, sg, bd=bd, t_new=t_new,
                            lam_init=lam_init, pages=math.gcd(DECODE_PAGES, page_table.shape[1]))

    n_even, n_odd = (depth + 1) // 2, depth // 2
    sc_zero = jnp.zeros((n_even, bp, SC_KERNEL - 1, sconv_w.shape[-1]), F32)
    conf_zero = jnp.zeros((n_odd, bp, CONF_KERNEL - 1, conf_dw.shape[-1]), F32)
    xp = x_prompt.reshape(bp * s_len, d)
    xs = x_sample.reshape(bd * t_new, d)
    new_p, new_s = [], []
    for l in range(depth):
        xp, st_p = _mixer(xp, l, mods_p[l][:3], pos_prompt, attend_prompt, sc_zero, conf_zero, p,
                          b=bp, s_len=s_len, prompt=True)
        xs, st_s = _mixer(xs, l, mods_s[l][:3], pos_sample, attend_sample, state_sconv, state_conf, p,
                          b=bd, s_len=t_new, prompt=False)
        new_p.append(st_p)
        new_s.append(st_s)
        (sh2p, sc2p, g2p), (sh2s, sc2s, g2s) = mods_p[l][3:], mods_s[l][3:]
        xp, xs = _moe_layer([(xp, sc2p, sh2p, g2p, min(256, bp * s_len)), (xs, sc2s, sh2s, g2s, bd * t_new)],
                            norm_ffn_g[l], p["router_w_t"][l], router_b[l],
                            moe_w_gu, moe_b_gu, moe_w_dn, moe_b_dn, layer=l, bm=MOE_BLOCK)

    def collect(new, b, s):
        even, odd = new[0::2], new[1::2]
        stack = lambda parts: parts[0][None] if len(parts) == 1 else jnp.stack(parts)
        return (stack([e[0] for e in even]), stack([e[1] for e in even]), stack([e[2] for e in even]),
                stack([o[0] for o in odd]))

    k_p, v_p, sc_p, cf_p = collect(new_p, bp, s_len)
    k_s, v_s, sc_s, cf_s = collect(new_s, bd, t_new)
    return (xp.reshape(bp, s_len, d), xs.reshape(bd, t_new, d), k_p, v_p, sc_p, cf_p, k_s, v_s, sc_s, cf_s)
```

```python
import functools
import math

import jax
import jax.numpy as jnp
from jax import lax
from jax.experimental import pallas as pl
from jax.experimental.pallas import tpu as pltpu

F32 = jnp.float32
BF16 = jnp.bfloat16
I32 = jnp.int32

EPS = 1e-6
HEAD_DIM = 64
V_DIM = 2 * HEAD_DIM
ROPE_DIM = HEAD_DIM // 4
ROPE_THETA = 500000.0
SC_KERNEL = 3
CONF_KERNEL = 31
N_EXPERTS = 32
TOP_K = 4
SWIGLU_LIMIT = 7.0
SWIGLU_ALPHA = 1.702
PAGE_SIZE = 128
DECODE_PAGES = 8
Q_SCALE = HEAD_DIM ** -0.5 * math.log2(math.e)

VMEM_LIMIT_BYTES = 56 * 1024 * 1024
SUBLANES = 8
MOE_ROW_ALIGN = 8
MOE_BLOCK = 512
MOE_SUB_BLOCK = 256


def _params(*sem):
    return pltpu.CompilerParams(dimension_semantics=sem, vmem_limit_bytes=VMEM_LIMIT_BYTES)


def _sigmoid(x):
    return 1.0 / (1.0 + jnp.exp(-x))


def _dot(a, b):
    return jnp.dot(a, b, preferred_element_type=F32)


def _dot_nt(a, b):
    return lax.dot_general(a, b, (((1,), (1,)), ((), ())), preferred_element_type=F32)


def _rmsnorm_mod(x, g, sc, sh):
    ms = jnp.mean(x * x, axis=-1, keepdims=True)
    return x * lax.rsqrt(ms + EPS) * g * (1.0 + sc) + sh


def _ada_kernel(c_ref, w_ref, b_ref, o_ref):
    c = c_ref[...]
    s = c * _sigmoid(c)
    o_ref[0] = jnp.dot(s, w_ref[0], precision=lax.Precision.HIGHEST, preferred_element_type=F32) + b_ref[0]


def _ada_mod(c_all, ada_w, ada_b):
    depth, d, n = ada_w.shape
    r = c_all.shape[0]
    tn = 1536
    return pl.pallas_call(
        _ada_kernel,
        out_shape=jax.ShapeDtypeStruct((depth, r, n), F32),
        grid=(depth, n // tn),
        in_specs=[pl.BlockSpec((r, d), lambda l, j: (0, 0)),
                  pl.BlockSpec((1, d, tn), lambda l, j: (l, 0, j)),
                  pl.BlockSpec((1, 1, tn), lambda l, j: (l, 0, j))],
        out_specs=pl.BlockSpec((1, r, tn), lambda l, j: (l, 0, j)),
        compiler_params=_params("arbitrary", "arbitrary"),
        name="ada_mod",
    )(c_all, ada_w, ada_b.reshape(depth, 1, n))


def _nmm_kernel(*refs, glu, has_bias):
    x_ref, g_ref, sc_ref, sh_ref = refs[:4]
    rest = list(refs[4:])
    w_refs = [rest.pop(0) for _ in range(2 if glu else 1)]
    b_refs = [rest.pop(0) for _ in range((2 if glu else 1) if has_bias else 0)]
    o_ref, h_scr = rest

    @pl.when(pl.program_id(1) == 0)
    def _():
        h = _rmsnorm_mod(x_ref[...], g_ref[...], sc_ref[0], sh_ref[0])
        h_scr[...] = h.astype(BF16)

    h = h_scr[...]
    accs = []
    for i, w_ref in enumerate(w_refs):
        a = _dot(h, w_ref[...])
        if has_bias:
            a = a + b_refs[i][...]
        accs.append(a)
    if glu:
        o_ref[...] = (accs[0] * _sigmoid(accs[1])).astype(o_ref.dtype)
    else:
        o_ref[...] = accs[0].astype(o_ref.dtype)


def _norm_mod_matmul(x, g, sc, sh, w, bias, *, tm, tn, glu=False, out_dtype=F32):
    t, d = x.shape
    n_out = w.shape[1] // 2 if glu else w.shape[1]
    ns, r, _ = sc.shape
    tps = (t // tm) // ns
    nj = n_out // tn
    in_specs = [pl.BlockSpec((tm, d), lambda i, j: (i, 0)),
                pl.BlockSpec((1, d), lambda i, j: (0, 0)),
                pl.BlockSpec((1, r, d), lambda i, j: (i // tps, 0, 0)),
                pl.BlockSpec((1, r, d), lambda i, j: (i // tps, 0, 0)),
                pl.BlockSpec((d, tn), lambda i, j: (0, j))]
    args = [x, g.reshape(1, d), sc, sh, w]
    if glu:
        in_specs.append(pl.BlockSpec((d, tn), lambda i, j: (0, j + nj)))
        args.append(w)
    if bias is not None:
        b2 = bias.reshape(1, -1)
        in_specs.append(pl.BlockSpec((1, tn), lambda i, j: (0, j)))
        args.append(b2)
        if glu:
            in_specs.append(pl.BlockSpec((1, tn), lambda i, j: (0, j + nj)))
            args.append(b2)
    return pl.pallas_call(
        functools.partial(_nmm_kernel, glu=glu, has_bias=bias is not None),
        out_shape=jax.ShapeDtypeStruct((t, n_out), out_dtype),
        grid=(t // tm, nj),
        in_specs=in_specs,
        out_specs=pl.BlockSpec((tm, tn), lambda i, j: (i, j)),
        scratch_shapes=[pltpu.VMEM((tm, d), BF16)],
        compiler_params=_params("arbitrary", "arbitrary"),
        name="norm_mod_matmul",
    )(*args)


def _group_mean_square(xh):
    r = lax.broadcasted_iota(I32, (V_DIM, V_DIM), 0) // HEAD_DIM
    c = lax.broadcasted_iota(I32, (V_DIM, V_DIM), 1) // HEAD_DIM
    ones_blk = jnp.where(r == c, 1.0, 0.0).astype(BF16)
    x2 = xh * xh
    hi = x2.astype(BF16)
    lo = (x2 - hi.astype(F32)).astype(BF16)
    return (_dot(hi, ones_blk) + _dot(lo, ones_blk)) * (1.0 / HEAD_DIM)


def _in_proj_kernel(x_ref, g_ref, sc_ref, sh_ref, w_ref, cos_ref, s1_ref, s2_ref, qg_ref, kg_ref,
                    qb_ref, krow_ref, kb_ref, vrow_ref, vb_ref, rest_ref, h_scr, *, steps_per_part):
    j = pl.program_id(1)
    spp = steps_per_part

    @pl.when(j == 0)
    def _():
        h_scr[...] = _rmsnorm_mod(x_ref[...], g_ref[...], sc_ref[0], sh_ref[0]).astype(BF16)

    acc = _dot(h_scr[...], w_ref[...])
    heads = acc.shape[1] // V_DIM

    def norm_rope(xh, g):
        y = xh * lax.rsqrt(_group_mean_square(xh) + EPS) * g
        return (y * cos_ref[...] + pltpu.roll(y, V_DIM - ROPE_DIM // 2, 1) * s1_ref[...]
                + pltpu.roll(y, ROPE_DIM // 2, 1) * s2_ref[...])

    @pl.when(j < spp)
    def _():
        for h in range(heads):
            sl = slice(h * V_DIM, (h + 1) * V_DIM)
            qb_ref[:, sl] = (norm_rope(acc[:, sl], qg_ref[...]) * Q_SCALE).astype(BF16)

    @pl.when((j >= spp) & (j < 2 * spp))
    def _():
        for h in range(heads):
            sl = slice(h * V_DIM, (h + 1) * V_DIM)
            kh = norm_rope(acc[:, sl], kg_ref[...])
            krow_ref[:, sl] = kh
            kb_ref[:, sl] = kh.astype(BF16)

    @pl.when((j >= 2 * spp) & (j < 3 * spp))
    def _():
        vrow_ref[...] = acc
        vb_ref[...] = acc.astype(BF16)

    @pl.when(j >= 3 * spp)
    def _():
        rest_ref[...] = acc


def _rope_tables(pos):
    half = ROPE_DIM // 2
    inv_freq = jnp.exp(-math.log(ROPE_THETA) * jnp.arange(half, dtype=F32) / half)
    ang = pos.astype(F32)[:, None] * inv_freq[None, :]
    cos, sin = jnp.cos(ang), jnp.sin(ang)
    p = pos.shape[0]
    ones = jnp.ones((p, HEAD_DIM - ROPE_DIM), F32)
    zeros = jnp.zeros((p, HEAD_DIM - ROPE_DIM), F32)
    z8 = jnp.zeros((p, half), F32)
    c64 = jnp.concatenate([cos, cos, ones], axis=1)
    s1_64 = jnp.concatenate([-sin, z8, zeros], axis=1)
    s2_64 = jnp.concatenate([z8, sin, zeros], axis=1)
    two = lambda a: jnp.concatenate([a, a], axis=1)
    return two(c64), two(s1_64), two(s2_64)


def _in_proj(x, g, sc, sh, w, tables, q_g, k_g, *, tm, tn):
    t, d = x.shape
    n_rest = w.shape[1] - 3 * d
    spp = d // tn
    nj = w.shape[1] // tn
    ns, r, _ = sc.shape
    tps = (t // tm) // ns
    npb = tables[0].shape[0] // tm
    part = lambda k: pl.BlockSpec((tm, tn), lambda i, j: (i, jnp.clip(j - k * spp, 0, spp - 1)))
    tab = pl.BlockSpec((tm, V_DIM), lambda i, j: (i % npb, 0))
    gspec = pl.BlockSpec((1, V_DIM), lambda i, j: (0, 0))
    mod = pl.BlockSpec((1, r, d), lambda i, j: (i // tps, 0, 0))
    g2 = lambda v: jnp.concatenate([v, v]).reshape(1, V_DIM)
    return pl.pallas_call(
        functools.partial(_in_proj_kernel, steps_per_part=spp),
        out_shape=[jax.ShapeDtypeStruct((t, d), BF16), jax.ShapeDtypeStruct((t, d), F32),
                   jax.ShapeDtypeStruct((t, d), BF16), jax.ShapeDtypeStruct((t, d), F32),
                   jax.ShapeDtypeStruct((t, d), BF16), jax.ShapeDtypeStruct((t, n_rest), F32)],
        grid=(t // tm, nj),
        in_specs=[pl.BlockSpec((tm, d), lambda i, j: (i, 0)), pl.BlockSpec((1, d), lambda i, j: (0, 0)), mod, mod,
                  pl.BlockSpec((d, tn), lambda i, j: (0, j)), tab, tab, tab, gspec, gspec],
        out_specs=[part(0), part(1), part(1), part(2), part(2),
                   pl.BlockSpec((tm, tn), lambda i, j: (i, jnp.clip(j - 3 * spp, 0, n_rest // tn - 1)))],
        scratch_shapes=[pltpu.VMEM((tm, d), BF16)],
        compiler_params=_params("arbitrary", "arbitrary"),
        name="in_proj",
    )(x, g.reshape(1, d), sc, sh, w, *tables, g2(q_g), g2(k_g))


def _diff_lambda(lq1, lk1, lq2, lk2, lam_init):
    a = jnp.sum(lq1 * lk1, axis=-1, keepdims=True)
    b = jnp.sum(lq2 * lk2, axis=-1, keepdims=True)
    return jnp.exp(a) - jnp.exp(b) + lam_init


def _subln(o, g, lam_init):
    ms = jnp.mean(o * o, axis=-1, keepdims=True)
    return o * lax.rsqrt(ms + EPS) * g * (1.0 - lam_init)


def _flash_kernel(q_ref, k_ref, v_ref, lq1, lk1, lq2, lk2, sg_ref, o_ref,
                  qm_scr, m_scr, acc_scr, *, lam_init, tq, hb):
    qi, ki, nk = pl.program_id(2), pl.program_id(3), pl.num_programs(3)
    nchunk = tq // V_DIM

    @pl.when(ki == 0)
    def _():
        for h in range(hb):
            q = q_ref[:, h * V_DIM:(h + 1) * V_DIM]
            lane = lax.broadcasted_iota(I32, q.shape, 1)
            zero = jnp.zeros_like(q)
            qm_scr[h, pl.ds(0, tq), :] = jnp.where(lane < HEAD_DIM, q, zero)
            qm_scr[h, pl.ds(tq, tq), :] = jnp.where(lane >= HEAD_DIM, q, zero)
        m_scr[...] = jnp.full(m_scr.shape, -jnp.inf, F32)
        acc_scr[...] = jnp.zeros(acc_scr.shape, F32)

    def tile_update(on_diagonal):
        for h in range(hb):
            k = k_ref[:, h * V_DIM:(h + 1) * V_DIM]
            v = v_ref[:, h * V_DIM:(h + 1) * V_DIM]
            v_ext = jnp.concatenate([v, jnp.ones_like(v)], axis=1)
            s = _dot_nt(qm_scr[h], k)
            if on_diagonal:
                row = lax.broadcasted_iota(I32, s.shape, 0) % tq
                col = lax.broadcasted_iota(I32, s.shape, 1)
                s = jnp.where(col <= row, s, -jnp.inf)
            m_old = m_scr[h]
            m_new = jnp.maximum(m_old, jnp.max(s, axis=-1, keepdims=True))
            alpha = jnp.exp2(m_old - m_new)
            p = jnp.concatenate([jnp.exp2(s[:, j * V_DIM:(j + 1) * V_DIM] - m_new) for j in range(nchunk)], axis=1)
            acc_scr[h] = jnp.concatenate([alpha, alpha], axis=1) * acc_scr[h] + _dot(p.astype(BF16), v_ext)
            m_scr[h] = m_new

    @pl.when(ki < qi)
    def _():
        tile_update(False)

    @pl.when(ki == qi)
    def _():
        tile_update(True)

    @pl.when(ki == nk - 1)
    def _():
        lam = _diff_lambda(lq1[...], lk1[...], lq2[...], lk2[...], lam_init)
        for h in range(hb):
            acc = acc_scr[h]
            o = acc[:tq, :V_DIM] / acc[:tq, V_DIM:] - lam * (acc[tq:, :V_DIM] / acc[tq:, V_DIM:])
            o_ref[:, h * V_DIM:(h + 1) * V_DIM] = _subln(o, sg_ref[...], lam_init).astype(o_ref.dtype)


def _flash_prompt(qb, kb, vb, lams, subln_g, *, b, s_len, d, lam_init, tq, hb):
    n_heads = d // V_DIM
    nq = s_len // tq
    w = hb * V_DIM
    lam_spec = pl.BlockSpec((1, HEAD_DIM), lambda bi, h, qi, ki: (0, 0))
    return pl.pallas_call(
        functools.partial(_flash_kernel, lam_init=lam_init, tq=tq, hb=hb),
        out_shape=jax.ShapeDtypeStruct((b * s_len, d), BF16),
        grid=(b, n_heads // hb, nq, nq),
        in_specs=[pl.BlockSpec((tq, w), lambda bi, h, qi, ki: (bi * nq + qi, h)),
                  pl.BlockSpec((tq, w), lambda bi, h, qi, ki: (bi * nq + jnp.minimum(ki, qi), h)),
                  pl.BlockSpec((tq, w), lambda bi, h, qi, ki: (bi * nq + jnp.minimum(ki, qi), h)),
                  lam_spec, lam_spec, lam_spec, lam_spec,
                  pl.BlockSpec((1, V_DIM), lambda bi, h, qi, ki: (0, 0))],
        out_specs=pl.BlockSpec((tq, w), lambda bi, h, qi, ki: (bi * nq + qi, h)),
        scratch_shapes=[pltpu.VMEM((hb, 2 * tq, V_DIM), BF16), pltpu.VMEM((hb, 2 * tq, V_DIM), F32),
                        pltpu.VMEM((hb, 2 * tq, 2 * V_DIM), F32)],
        compiler_params=_params("arbitrary", "arbitrary", "arbitrary", "arbitrary"),
        name="flash_prompt",
    )(qb, kb, vb, *[x.reshape(1, HEAD_DIM) for x in lams], subln_g.reshape(1, V_DIM))


def _decode_kernel(pt_ref, q_ref, kn_ref, vn_ref, lq1, lk1, lq2, lk2, sg_ref, *refs,
                   lam_init, n_heads, pages, t_new, page_base):
    ck_hbm, cv_hbm, o_ref, kbuf, vbuf, sems, q_scr, m_scr, l_scr, acc_scr = refs
    bi, ci, nc = pl.program_id(0), pl.program_id(1), pl.num_programs(1)
    step, n_steps = bi * nc + ci, pl.num_programs(0) * nc
    rows = 2 * t_new * n_heads
    rows_per_page = PAGE_SIZE * n_heads
    row_head = lax.broadcasted_iota(I32, (rows, V_DIM), 0) // (2 * t_new)
    lane = lax.broadcasted_iota(I32, (rows, V_DIM), 1)

    def page_copies(g, slot):
        b, c = g // nc, g % nc
        copies = []
        for p in range(pages):
            page = page_base + pt_ref[b, c * pages + p]
            dst = pl.ds(p * rows_per_page, rows_per_page)
            copies.append(pltpu.make_async_copy(ck_hbm.at[page], kbuf.at[slot, dst], sems.at[slot, 0]))
            copies.append(pltpu.make_async_copy(cv_hbm.at[page], vbuf.at[slot, dst], sems.at[slot, 1]))
        return copies

    slot = step % 2

    @pl.when(step == 0)
    def _():
        for cp in page_copies(step, slot):
            cp.start()

    @pl.when(step + 1 < n_steps)
    def _():
        for cp in page_copies(step + 1, 1 - slot):
            cp.start()

    @pl.when(ci == 0)
    def _():
        comp = (lax.broadcasted_iota(I32, (rows, V_DIM), 0) // t_new) % 2
        q = q_ref[0]
        q_scr[...] = jnp.where(lane // HEAD_DIM == comp, q, jnp.zeros_like(q))
        n_new = t_new * n_heads
        s = _dot_nt(q_scr[...], kn_ref[0].astype(BF16))
        r = lax.broadcasted_iota(I32, s.shape, 0)
        c = lax.broadcasted_iota(I32, s.shape, 1)
        visible = (c % n_heads == r // (2 * t_new)) & (c // n_heads <= r % t_new)
        s = jnp.where(visible, s, -jnp.inf)
        m = jnp.max(s, axis=-1, keepdims=True)
        p = jnp.exp2(s - m)
        m_scr[...] = jnp.broadcast_to(m, m_scr.shape)
        l_scr[...] = jnp.broadcast_to(jnp.sum(p, axis=-1, keepdims=True), l_scr.shape)
        acc_scr[...] = _dot(p.astype(BF16), vn_ref[0].astype(BF16))

    for cp in page_copies(step, slot):
        cp.wait()
    k2 = kbuf[slot].astype(BF16)
    v2 = vbuf[slot].astype(BF16)
    s = _dot_nt(q_scr[...], k2)
    bias = jnp.where(lane % n_heads == row_head, 0.0, -jnp.inf)
    chunks = [s[:, j * V_DIM:(j + 1) * V_DIM] + bias for j in range(s.shape[1] // V_DIM)]
    m_old = m_scr[...]
    m_new = jnp.maximum(m_old, jnp.max(functools.reduce(jnp.maximum, chunks), axis=-1, keepdims=True))
    alpha = jnp.exp2(m_old - m_new)
    ps = [jnp.exp2(ch - m_new) for ch in chunks]
    l_scr[...] = alpha * l_scr[...] + jnp.sum(functools.reduce(jnp.add, ps), axis=-1, keepdims=True)
    acc_scr[...] = alpha * acc_scr[...] + _dot(jnp.concatenate(ps, axis=1).astype(BF16), v2)
    m_scr[...] = m_new

    @pl.when(ci == nc - 1)
    def _():
        lam = _diff_lambda(lq1[...], lk1[...], lq2[...], lk2[...], lam_init)
        o = acc_scr[...] / l_scr[...]
        o = o - lam * pltpu.roll(o, rows - t_new, 0)
        o_ref[0] = _subln(o, sg_ref[...], lam_init)


def _decode_attn(qb, k_new, v_new, cache_k, cache_v, li, page_table, lams, subln_g, *, bd, t_new, lam_init, pages):
    d = qb.shape[1]
    n_heads = d // V_DIM
    n_pool = cache_k.shape[1]
    rows_per_page = PAGE_SIZE * n_heads
    ck = cache_k.reshape(-1, rows_per_page, V_DIM)
    cv = cache_v.reshape(-1, rows_per_page, V_DIM)
    n_pages = page_table.shape[1]
    nc = n_pages // pages
    rows = 2 * t_new * n_heads
    n_new = t_new * n_heads
    q2 = jnp.broadcast_to(qb.reshape(bd, t_new, n_heads, 1, V_DIM).transpose(0, 2, 3, 1, 4),
                          (bd, n_heads, 2, t_new, V_DIM)).reshape(bd, rows, V_DIM)

    new_spec = pl.BlockSpec((1, n_new, V_DIM), lambda b, c, pt: (b, 0, 0))
    lam_spec = pl.BlockSpec((1, HEAD_DIM), lambda b, c, pt: (0, 0))
    chunk_rows = pages * rows_per_page
    grid_spec = pltpu.PrefetchScalarGridSpec(
        num_scalar_prefetch=1,
        grid=(bd, nc),
        in_specs=[pl.BlockSpec((1, rows, V_DIM), lambda b, c, pt: (b, 0, 0)), new_spec, new_spec,
                  lam_spec, lam_spec, lam_spec, lam_spec, pl.BlockSpec((1, V_DIM), lambda b, c, pt: (0, 0)),
                  pl.BlockSpec(memory_space=pl.ANY), pl.BlockSpec(memory_space=pl.ANY)],
        out_specs=pl.BlockSpec((1, rows, V_DIM), lambda b, c, pt: (b, 0, 0)),
        scratch_shapes=[pltpu.VMEM((2, chunk_rows, V_DIM), F32), pltpu.VMEM((2, chunk_rows, V_DIM), F32),
                        pltpu.SemaphoreType.DMA((2, 2)),
                        pltpu.VMEM((rows, V_DIM), BF16), pltpu.VMEM((rows, V_DIM), F32),
                        pltpu.VMEM((rows, V_DIM), F32), pltpu.VMEM((rows, V_DIM), F32)])
    o2 = pl.pallas_call(
        functools.partial(_decode_kernel, lam_init=lam_init, n_heads=n_heads, pages=pages, t_new=t_new,
                          page_base=li * n_pool),
        out_shape=jax.ShapeDtypeStruct((bd, rows, V_DIM), F32),
        grid_spec=grid_spec,
        compiler_params=_params("arbitrary", "arbitrary"),
        name="decode_attn",
    )(page_table, q2, k_new.reshape(bd, n_new, V_DIM), v_new.reshape(bd, n_new, V_DIM),
      *[x.reshape(1, HEAD_DIM) for x in lams], subln_g.reshape(1, V_DIM), ck, cv)
    o = o2.reshape(bd, n_heads, 2, t_new, V_DIM)[:, :, 0].transpose(0, 2, 1, 3)
    return o.reshape(bd * t_new, d).astype(BF16)


def _layernorm_silu(y, g, b):
    mu = jnp.mean(y, axis=-1, keepdims=True)
    yc = y - mu
    var = jnp.mean(yc * yc, axis=-1, keepdims=True)
    z = yc * lax.rsqrt(var + EPS) * g + b
    return z * _sigmoid(z)


CONV_ROW_CHUNK = 128
CONV_LANE_CHUNK = 128


def _depthwise_taps(ext, w_ref, y_scr, *, taps, base, tt):
    c = y_scr.shape[1]
    rc, cc = CONV_ROW_CHUNK, min(CONV_LANE_CHUNK, c)
    by_residue = {}
    for j in range(taps):
        by_residue.setdefault((base + j) % SUBLANES, []).append(j)

    def row_chunk(i, carry):
        row0 = pl.multiple_of(i * rc, rc)
        for c0 in range(0, c, cc):
            y = None
            for r, js in sorted(by_residue.items()):
                acc = None
                for j in js:
                    aligned = base + j - r
                    term = (w_ref[pl.ds(j, 1), pl.ds(c0, cc)]
                            * ext[pl.ds(row0 + aligned, rc + SUBLANES), pl.ds(c0, cc)])
                    acc = term if acc is None else acc + term
                part = acc[r:r + rc]
                y = part if y is None else y + part
            y_scr[pl.ds(row0, rc), pl.ds(c0, cc)] = y
        return carry

    lax.fori_loop(0, tt // rc, row_chunk, 0)


def _conv_prompt_kernel(*refs, taps, halo, tt, gated, ln):
    refs = list(refs)
    u_ref, uh_ref, prev_ref = refs[:3]
    del refs[:3]
    if gated:
        gc_ref, gch_ref, gb_ref = refs[:3]
        del refs[:3]
    w_ref = refs.pop(0)
    if ln:
        cb_ref, lg_ref, lb_ref = refs[:3]
        del refs[:3]
    o_ref = refs.pop(0)
    if gated:
        tail_ref = refs.pop(0)
    ext, y_scr = refs
    ti, nt = pl.program_id(1), pl.num_programs(1)

    cur = u_ref[...]
    if gated:
        cur = cur * gc_ref[...]
    ext[pl.ds(halo, tt), :] = cur

    @pl.when(ti == 0)
    def _():
        ext[pl.ds(0, halo), :] = prev_ref[0]

    @pl.when(ti > 0)
    def _():
        hv = uh_ref[...]
        if gated:
            hv = hv * gch_ref[...]
        ext[pl.ds(0, halo), :] = hv

    ext[pl.ds(halo + tt, SUBLANES), :] = jnp.zeros((SUBLANES, ext.shape[1]), F32)
    _depthwise_taps(ext, w_ref, y_scr, taps=taps, base=halo - (taps - 1), tt=tt)
    y = y_scr[...]
    if ln:
        y = _layernorm_silu(y + cb_ref[...], lg_ref[...], lb_ref[...])
    if gated:
        y = gb_ref[...] * y

        @pl.when(ti == nt - 1)
        def _():
            tail_ref[0] = ext[pl.ds(halo + tt - SUBLANES, SUBLANES), :]
    o_ref[...] = y.astype(o_ref.dtype)


def _conv_prompt(src, ucol, prev, w, *, b, s_len, c, tt, gate_cols=None, ln_params=None):
    taps = w.shape[0]
    halo = SUBLANES if taps - 1 <= SUBLANES else 32
    nt = s_len // tt
    hpt = tt // halo
    prev_p = jnp.pad(prev, ((0, 0), (halo - (taps - 1), 0), (0, 0)))
    cur = lambda col: pl.BlockSpec((tt, c), lambda bi, ti: (bi * nt + ti, col))
    hal = lambda col: pl.BlockSpec((halo, c), lambda bi, ti: (jnp.maximum((bi * nt + ti) * hpt - 1, 0), col))
    vec = pl.BlockSpec((1, c), lambda bi, ti: (0, 0))
    in_specs = [cur(ucol), hal(ucol), pl.BlockSpec((1, halo, c), lambda bi, ti: (bi, 0, 0))]
    args = [src, src, prev_p]
    gated = gate_cols is not None
    if gated:
        cc, cb = gate_cols
        in_specs += [cur(cc), hal(cc), cur(cb)]
        args += [src, src, src]
    in_specs.append(pl.BlockSpec((taps, c), lambda bi, ti: (0, 0)))
    args.append(w)
    if ln_params is not None:
        in_specs += [vec, vec, vec]
        args += [p.reshape(1, c) for p in ln_params]
    out_shape = [jax.ShapeDtypeStruct((b * s_len, c), BF16)]
    out_specs = [pl.BlockSpec((tt, c), lambda bi, ti: (bi * nt + ti, 0))]
    if gated:
        out_shape.append(jax.ShapeDtypeStruct((b, SUBLANES, c), F32))
        out_specs.append(pl.BlockSpec((1, SUBLANES, c), lambda bi, ti: (bi, 0, 0)))
    res = pl.pallas_call(
        functools.partial(_conv_prompt_kernel, taps=taps, halo=halo, tt=tt, gated=gated, ln=ln_params is not None),
        out_shape=out_shape,
        grid=(b, nt),
        in_specs=in_specs,
        out_specs=out_specs,
        scratch_shapes=[pltpu.VMEM((halo + tt + SUBLANES, c), F32), pltpu.VMEM((tt, c), F32)],
        compiler_params=_params("arbitrary", "arbitrary"),
        name="conv_prompt",
    )(*args)
    return res if gated else res[0]


def _conv_sample_kernel(*refs, taps, s_new, gated, ln):
    refs = list(refs)
    prev_ref, u_ref = refs[:2]
    del refs[:2]
    if gated:
        gc_ref, gb_ref = refs[:2]
        del refs[:2]
    w_ref = refs.pop(0)
    if ln:
        cb_ref, lg_ref, lb_ref = refs[:3]
        del refs[:3]
    o_ref, cu_ref, ext = refs
    for j in range(taps - 1):
        ext[j] = prev_ref[j]
    for s in range(s_new):
        cu = u_ref[s]
        if gated:
            cu = cu * gc_ref[s]
        cu_ref[s] = cu
        ext[taps - 1 + s] = cu
    for s in range(s_new):
        y = w_ref[0] * ext[s]
        for j in range(1, taps):
            y = y + w_ref[j] * ext[s + j]
        if ln:
            y = _layernorm_silu(y + cb_ref[...], lg_ref[...], lb_ref[...])
        if gated:
            y = gb_ref[s] * y
        o_ref[s] = y.astype(o_ref.dtype)


def _conv_sample(u, prev, w, *, gates=None, ln_params=None):
    s_new, bd, c = u.shape
    taps = w.shape[0]
    args = [prev, u]
    gated = gates is not None
    if gated:
        args += list(gates)
    args.append(w.reshape(taps, 1, c))
    if ln_params is not None:
        args += [p.reshape(1, c) for p in ln_params]
    return pl.pallas_call(
        functools.partial(_conv_sample_kernel, taps=taps, s_new=s_new, gated=gated, ln=ln_params is not None),
        out_shape=[jax.ShapeDtypeStruct((s_new, bd, c), BF16), jax.ShapeDtypeStruct((s_new, bd, c), F32)],
        scratch_shapes=[pltpu.VMEM((taps - 1 + s_new, bd, c), F32)],
        compiler_params=pltpu.CompilerParams(vmem_limit_bytes=VMEM_LIMIT_BYTES),
        name="conv_sample",
    )(*args)


def _mm_res_kernel(*refs, n_in, has_bias):
    a_refs, w_refs = refs[:n_in], refs[n_in:2 * n_in]
    rest = list(refs[2 * n_in:])
    b_ref = rest.pop(0) if has_bias else None
    x_ref, gate_ref, o_ref = rest
    acc = _dot(a_refs[0][...], w_refs[0][...])
    for a_ref, w_ref in zip(a_refs[1:], w_refs[1:]):
        acc = acc + _dot(a_ref[...], w_ref[...])
    if has_bias:
        acc = acc + b_ref[...]
    o_ref[...] = x_ref[...] + gate_ref[0] * acc


def _matmul_residual(a_list, w_list, bias, x, gate, *, tm, tn):
    t, d = x.shape
    ns, r, _ = gate.shape
    tps = (t // tm) // ns
    n_in = len(a_list)
    in_specs = [pl.BlockSpec((tm, a.shape[1]), lambda i, j: (i, 0)) for a in a_list]
    in_specs += [pl.BlockSpec((w.shape[0], tn), lambda i, j: (0, j)) for w in w_list]
    args = list(a_list) + list(w_list)
    if bias is not None:
        in_specs.append(pl.BlockSpec((1, tn), lambda i, j: (0, j)))
        args.append(bias.reshape(1, d))
    in_specs += [pl.BlockSpec((tm, tn), lambda i, j: (i, j)),
                 pl.BlockSpec((1, r, tn), lambda i, j: (i // tps, 0, j))]
    args += [x, gate]
    return pl.pallas_call(
        functools.partial(_mm_res_kernel, n_in=n_in, has_bias=bias is not None),
        out_shape=jax.ShapeDtypeStruct((t, d), F32),
        grid=(t // tm, d // tn),
        in_specs=in_specs,
        out_specs=pl.BlockSpec((tm, tn), lambda i, j: (i, j)),
        compiler_params=_params("arbitrary", "arbitrary"),
        name="matmul_residual",
    )(*args)


def _router_kernel(x_ref, g_ref, sc_ref, sh_ref, rw_ref, rb_ref,
                   hb_ref, tw_ref, pos_ref, sizes_ref, *, tm):
    h = _rmsnorm_mod(x_ref[...], g_ref[...], sc_ref[0], sh_ref[0])
    hb_ref[...] = h.astype(BF16)
    logits = lax.dot_general(rw_ref[...], h, (((1,), (1,)), ((), ())),
                             precision=lax.Precision.HIGHEST, preferred_element_type=F32) + rb_ref[...]
    e_iota = lax.broadcasted_iota(I32, logits.shape, 0)
    work = logits
    ids, vals = [], []
    multi_hot = jnp.zeros(logits.shape, F32)
    for _ in range(TOP_K):
        m = jnp.max(work, axis=0, keepdims=True)
        idx = jnp.min(jnp.where(work == m, e_iota, N_EXPERTS), axis=0, keepdims=True)
        sel = e_iota == idx
        work = jnp.where(sel, -jnp.inf, work)
        multi_hot = multi_hot + jnp.where(sel, 1.0, 0.0)
        ids.append(idx)
        vals.append(m)
    ex = [jnp.exp(v - vals[0]) for v in vals]
    den = ex[0] + ex[1] + ex[2] + ex[3]
    r = lax.broadcasted_iota(I32, (tm, tm), 0)
    c = lax.broadcasted_iota(I32, (tm, tm), 1)
    before = jnp.where(r < c, 1.0, 0.0).astype(BF16)
    earlier = _dot(multi_hot.astype(BF16), before)
    cnt = jnp.sum(multi_hot, axis=1, keepdims=True)
    sizes = jnp.ceil(cnt * (1.0 / MOE_ROW_ALIGN)) * MOE_ROW_ALIGN
    er = lax.broadcasted_iota(I32, (N_EXPERTS, N_EXPERTS), 0)
    ec = lax.broadcasted_iota(I32, (N_EXPERTS, N_EXPERTS), 1)
    lower = jnp.where(ec < er, 1.0, 0.0).astype(BF16)
    run_start = _dot(lower, jnp.broadcast_to(sizes, (N_EXPERTS, V_DIM)).astype(BF16))[:, :1]
    row_in_tile = run_start + earlier
    for k in range(TOP_K):
        tw_ref[pl.ds(k, 1), :] = ex[k] / den
        pos_ref[pl.ds(k, 1), :] = jnp.sum(jnp.where(e_iota == ids[k], row_in_tile, 0.0),
                                          axis=0, keepdims=True).astype(I32)
    sizes_ref[0] = sizes.astype(I32)


def _router(x, g, sc, sh, rw_t, rb, *, tm):
    t, d = x.shape
    ns, r, _ = sc.shape
    nt = t // tm
    tps = nt // ns
    mod = pl.BlockSpec((1, r if r == 1 else tm, d), lambda i: (i // tps, 0 if r == 1 else i % tps, 0))
    lane4 = pl.BlockSpec((TOP_K, tm), lambda i: (0, i))
    return pl.pallas_call(
        functools.partial(_router_kernel, tm=tm),
        out_shape=[jax.ShapeDtypeStruct((t, d), BF16), jax.ShapeDtypeStruct((TOP_K, t), F32),
                   jax.ShapeDtypeStruct((TOP_K, t), I32), jax.ShapeDtypeStruct((nt, N_EXPERTS, 1), I32)],
        grid=(nt,),
        in_specs=[pl.BlockSpec((tm, d), lambda i: (i, 0)), pl.BlockSpec((1, d), lambda i: (0, 0)), mod, mod,
                  pl.BlockSpec((N_EXPERTS, d), lambda i: (0, 0)), pl.BlockSpec((N_EXPERTS, 1), lambda i: (0, 0))],
        out_specs=[pl.BlockSpec((tm, d), lambda i: (i, 0)), lane4, lane4,
                   pl.BlockSpec((1, N_EXPERTS, 1), lambda i: (i, 0, 0))],
        compiler_params=_params("arbitrary"),
        name="moe_router",
    )(x, g.reshape(1, d), sc, sh, rw_t, rb.reshape(N_EXPERTS, 1))


def _for_each_run(sizes_ref, offs_ref, tile, action):
    def one(e, local):
        n = pl.multiple_of(sizes_ref[tile * N_EXPERTS + e], MOE_ROW_ALIGN)
        off = pl.multiple_of(offs_ref[tile * N_EXPERTS + e], MOE_ROW_ALIGN)
        loc = pl.multiple_of(local, MOE_ROW_ALIGN)

        @pl.when(n > 0)
        def _():
            action(loc, off, n)
        return local + n

    lax.fori_loop(0, N_EXPERTS, one, 0)


def _zero_fill_unused(gap_ref, nu_ref, xb_hbm, zero_scr, sem, *, bm, nb):
    zero_scr[...] = jnp.zeros(zero_scr.shape, F32)

    def gap_copy(e):
        n = pl.multiple_of(gap_ref[N_EXPERTS + e], MOE_ROW_ALIGN)
        off = pl.multiple_of(gap_ref[e], MOE_ROW_ALIGN)
        return n, pltpu.make_async_copy(zero_scr.at[pl.ds(0, n)], xb_hbm.at[pl.ds(off, n)], sem)

    def blk_copy(b):
        return pltpu.make_async_copy(zero_scr, xb_hbm.at[pl.ds(pl.multiple_of(b * bm, bm), bm)], sem)

    def each_gap(action):
        def body(e, carry):
            n, cp = gap_copy(e)

            @pl.when(n > 0)
            def _():
                action(cp)
            return carry
        lax.fori_loop(0, N_EXPERTS, body, 0)

    def each_blk(action):
        def body(b, carry):
            action(blk_copy(b))
            return carry
        lax.fori_loop(nu_ref[0], nb, body, 0)

    each_gap(lambda cp: cp.start())
    each_blk(lambda cp: cp.start())
    each_gap(lambda cp: cp.wait())
    each_blk(lambda cp: cp.wait())


def _moe_rows(tm):
    return tm * TOP_K + N_EXPERTS * MOE_ROW_ALIGN


def _dispatch_kernel(sizes_ref, offs_ref, gap_ref, nu_ref, *refs, tile_ranges, bm, nb):
    n_groups = len(tile_ranges)
    hb_refs, pos_refs = refs[0:2 * n_groups:2], refs[1:2 * n_groups:2]
    xb_hbm, sorted_scr, zero_scr, sems = refs[2 * n_groups:]
    tile, n_tiles = pl.program_id(0), pl.num_programs(0)
    slot = tile % 2

    for hb_ref, pos_ref, (first, last) in zip(hb_refs, pos_refs, tile_ranges):
        @pl.when((tile >= first) & (tile < last))
        def _():
            tm = hb_ref.shape[0]
            rows = _moe_rows(tm)
            r_iota = lax.broadcasted_iota(I32, (rows, tm), 0)
            hit = r_iota == pos_ref[pl.ds(0, 1), :]
            for k in range(1, TOP_K):
                hit = hit | (r_iota == pos_ref[pl.ds(k, 1), :])
            perm = jnp.where(hit, 1.0, 0.0).astype(BF16)
            sorted_scr[slot, pl.ds(0, rows), :] = _dot(perm, hb_ref[...])

    def run_copy(s):
        def make(loc, off, n):
            return pltpu.make_async_copy(sorted_scr.at[s, pl.ds(loc, n)], xb_hbm.at[pl.ds(off, n)], sems.at[s])
        return make

    _for_each_run(sizes_ref, offs_ref, tile, lambda *a: run_copy(slot)(*a).start())

    @pl.when(tile > 0)
    def _():
        _for_each_run(sizes_ref, offs_ref, tile - 1, lambda *a: run_copy(1 - slot)(*a).wait())

    @pl.when(tile == n_tiles - 1)
    def _():
        _for_each_run(sizes_ref, offs_ref, tile, lambda *a: run_copy(slot)(*a).wait())
        _zero_fill_unused(gap_ref, nu_ref, xb_hbm, zero_scr, sems.at[0], bm=bm, nb=nb)


def _dispatch(groups, sizes, offs, gaps, n_used, *, cap, bm):
    d = groups[0][0].shape[1]
    in_specs, args, tile_ranges = [], [], []
    first = 0
    for hb, pos_t, tm in groups:
        nt = hb.shape[0] // tm
        local = lambda i, first=first, nt=nt: jnp.clip(i - first, 0, nt - 1)
        in_specs += [pl.BlockSpec((tm, d), lambda i, *_, local=local: (local(i), 0)),
                     pl.BlockSpec((TOP_K, tm), lambda i, *_, local=local: (0, local(i)))]
        args += [hb, pos_t]
        tile_ranges.append((first, first + nt))
        first += nt
    rows = max(_moe_rows(tm) for _, _, tm in groups)
    grid_spec = pltpu.PrefetchScalarGridSpec(
        num_scalar_prefetch=4,
        grid=(first,),
        in_specs=in_specs,
        out_specs=pl.BlockSpec(memory_space=pl.ANY),
        scratch_shapes=[pltpu.VMEM((2, rows, d), F32), pltpu.VMEM((bm, d), F32), pltpu.SemaphoreType.DMA((2,))])
    return pl.pallas_call(
        functools.partial(_dispatch_kernel, tile_ranges=tuple(tile_ranges), bm=bm, nb=cap // bm),
        out_shape=jax.ShapeDtypeStruct((cap, d), F32),
        grid_spec=grid_spec,
        compiler_params=_params("arbitrary"),
        name="moe_dispatch",
    )(sizes, offs, gaps, n_used, *args)


def _expert_kernel(be_ref, bv_ref, nx_ref, nu_ref, x_ref, bgu_ref, bdn_ref, wgu_hbm, wdn_hbm, y_ref,
                   stage_gu, stage_dn, wgu_scr, wdn_scr, sems, *, d_ff, sub, layer):
    b = pl.program_id(0)
    e = be_ref[b]
    prev_e = be_ref[jnp.maximum(b - 1, 0)]

    def fetch(expert):
        return (pltpu.make_async_copy(wgu_hbm.at[layer, expert], stage_gu, sems.at[0]),
                pltpu.make_async_copy(wdn_hbm.at[layer, expert], stage_dn, sems.at[1]))

    @pl.when(b == 0)
    def _():
        for cp in fetch(e):
            cp.start()

    @pl.when((b == 0) | (e != prev_e))
    def _():
        for cp in fetch(e):
            cp.wait()
        wgu_scr[...] = stage_gu[...].astype(BF16)
        wdn_scr[...] = stage_dn[...].astype(BF16)
        nxt = nx_ref[b]

        @pl.when(nxt >= 0)
        def _():
            for cp in fetch(nxt):
                cp.start()

    valid = bv_ref[b]
    for start in range(0, x_ref.shape[0], sub):
        rows = pl.ds(start, sub)

        @pl.when(valid > start)
        def _():
            x = x_ref[rows, :]
            row = lax.broadcasted_iota(I32, x.shape, 0) + start
            x = jnp.where(row < valid, x, 0.0).astype(BF16)
            hg = _dot(x, wgu_scr[...]) + bgu_ref[0, 0]
            g = jnp.minimum(hg[:, :d_ff], SWIGLU_LIMIT)
            lin = jnp.clip(hg[:, d_ff:], -SWIGLU_LIMIT, SWIGLU_LIMIT)
            act = g * _sigmoid(SWIGLU_ALPHA * g) * (lin + 1.0)
            y_ref[rows, :] = _dot(act.astype(BF16), wdn_scr[...]) + bdn_ref[0, 0]

        @pl.when(valid <= start)
        def _():
            y_ref[rows, :] = jnp.zeros((sub, y_ref.shape[1]), F32)


def _experts(xb, blk_e, blk_valid, blk_next_e, n_used, w_gu, b_gu, w_dn, b_dn, *, layer, bm):
    cap, d = xb.shape
    depth, n_e, _, two_ff = w_gu.shape
    d_ff = two_ff // 2
    nb = cap // bm
    grid_spec = pltpu.PrefetchScalarGridSpec(
        num_scalar_prefetch=4,
        grid=(nb,),
        in_specs=[pl.BlockSpec((bm, d), lambda b, be, bv, nx, nu: (jnp.minimum(b, nu[0] - 1), 0)),
                  pl.BlockSpec((1, 1, 1, two_ff), lambda b, be, bv, nx, nu: (layer, be[b], 0, 0)),
                  pl.BlockSpec((1, 1, 1, d), lambda b, be, bv, nx, nu: (layer, be[b], 0, 0)),
                  pl.BlockSpec(memory_space=pl.ANY), pl.BlockSpec(memory_space=pl.ANY)],
        out_specs=pl.BlockSpec((bm, d), lambda b, be, bv, nx, nu: (b, 0)),
        scratch_shapes=[pltpu.VMEM((d, two_ff), F32), pltpu.VMEM((d_ff, d), F32),
                        pltpu.VMEM((d, two_ff), BF16), pltpu.VMEM((d_ff, d), BF16), pltpu.SemaphoreType.DMA((2,))])
    return pl.pallas_call(
        functools.partial(_expert_kernel, d_ff=d_ff, sub=min(MOE_SUB_BLOCK, bm), layer=layer),
        out_shape=jax.ShapeDtypeStruct((cap, d), F32),
        grid_spec=grid_spec,
        compiler_params=_params("arbitrary"),
        name="moe_experts",
    )(blk_e, blk_valid, blk_next_e, n_used, xb, b_gu.reshape(depth, n_e, 1, two_ff), b_dn.reshape(depth, n_e, 1, d),
      w_gu, w_dn)


def _combine_kernel(sizes_ref, offs_ref, x_ref, gate_ref, pos_ref, tw_ref, yb_hbm, o_ref, ys_scr, sems,
                    *, rows, first_tile):
    tm = x_ref.shape[0]
    step, n_steps = pl.program_id(0), pl.num_programs(0)
    tile = step + first_tile
    slot = step % 2

    def run_copy(s):
        def make(loc, off, n):
            return pltpu.make_async_copy(yb_hbm.at[pl.ds(off, n)], ys_scr.at[s, pl.ds(loc, n)], sems.at[s])
        return make

    @pl.when(step == 0)
    def _():
        ys_scr[...] = jnp.zeros(ys_scr.shape, F32)
        _for_each_run(sizes_ref, offs_ref, tile, lambda *a: run_copy(slot)(*a).start())

    @pl.when(step + 1 < n_steps)
    def _():
        _for_each_run(sizes_ref, offs_ref, tile + 1, lambda *a: run_copy(1 - slot)(*a).start())

    _for_each_run(sizes_ref, offs_ref, tile, lambda *a: run_copy(slot)(*a).wait())

    r_iota = lax.broadcasted_iota(I32, (tm, rows), 1)
    wmat = jnp.zeros((tm, rows), F32)
    for k in range(TOP_K):
        wmat = wmat + jnp.where(r_iota == pos_ref[:, pl.ds(k, 1)], tw_ref[:, pl.ds(k, 1)], 0.0)
    y = _dot(wmat.astype(BF16), ys_scr[slot].astype(BF16))
    o_ref[...] = x_ref[...] + gate_ref[0] * y


def _combine(x, gate, pos, tw, yb, sizes, offs, *, tm, first_tile):
    t, d = x.shape
    ns, r, _ = gate.shape
    nt = t // tm
    tps = nt // ns
    rows = _moe_rows(tm)
    mod = pl.BlockSpec((1, r if r == 1 else tm, d),
                       lambda i, s, o: (i // tps, 0 if r == 1 else i % tps, 0))
    col4 = pl.BlockSpec((tm, TOP_K), lambda i, s, o: (i, 0))
    grid_spec = pltpu.PrefetchScalarGridSpec(
        num_scalar_prefetch=2,
        grid=(nt,),
        in_specs=[pl.BlockSpec((tm, d), lambda i, s, o: (i, 0)), mod, col4, col4,
                  pl.BlockSpec(memory_space=pl.ANY)],
        out_specs=pl.BlockSpec((tm, d), lambda i, s, o: (i, 0)),
        scratch_shapes=[pltpu.VMEM((2, rows, d), F32), pltpu.SemaphoreType.DMA((2,))])
    return pl.pallas_call(
        functools.partial(_combine_kernel, rows=rows, first_tile=first_tile),
        out_shape=jax.ShapeDtypeStruct((t, d), F32),
        grid_spec=grid_spec,
        compiler_params=_params("arbitrary"),
        name="moe_combine",
    )(sizes, offs, x, gate, pos, tw, yb)


def _round_up(x, m):
    return (x + m - 1) // m * m


def _moe_layer(groups, g, rw_t, rb, w_gu, b_gu, w_dn, b_dn, *, layer, bm):
    routed = [_router(x, g, sc, sh, rw_t, rb, tm=tm) for x, sc, sh, _, tm in groups]
    sizes = jnp.concatenate([r[3].reshape(-1, N_EXPERTS) for r in routed], axis=0)
    nt = sizes.shape[0]
    rows_e = jnp.sum(sizes, axis=0)
    region = _round_up(rows_e, bm)
    region_end = jnp.cumsum(region)
    region_base = region_end - region
    offs = region_base[None, :] + jnp.cumsum(sizes, axis=0) - sizes
    t_all = sum(x.shape[0] for x, *_ in groups)
    cap = _round_up(t_all * TOP_K + nt * N_EXPERTS * (MOE_ROW_ALIGN - 1), bm) + N_EXPERTS * bm
    nb = cap // bm
    blk_start = jnp.arange(nb, dtype=I32) * bm
    n_used = (region_end[-1] // bm).astype(I32).reshape(1)
    blk_e = jnp.sum((region_end[None, :] <= blk_start[:, None]).astype(I32), axis=1)
    last_e = jnp.sum((region_end < region_end[-1]).astype(I32))
    blk_e = jnp.where(blk_start < region_end[-1], blk_e, last_e).astype(I32)
    onehot = blk_e[:, None] == jnp.arange(N_EXPERTS, dtype=I32)[None, :]
    rows_end = jnp.sum(jnp.where(onehot, (region_base + rows_e)[None, :], 0), axis=1)
    blk_valid = jnp.clip(rows_end - blk_start, 0, bm).astype(I32)
    next_blk = jnp.sum(jnp.where(onehot, region_end[None, :], 0), axis=1) // bm
    next_onehot = next_blk[:, None] == jnp.arange(nb, dtype=I32)[None, :]
    blk_next_e = jnp.where(next_blk < n_used[0], jnp.sum(jnp.where(next_onehot, blk_e[None, :], 0), axis=1), -1)
    blk_next_e = blk_next_e.astype(I32)
    sizes_f = sizes.reshape(-1).astype(I32)
    offs_f = offs.reshape(-1).astype(I32)
    gaps = jnp.concatenate([region_base + rows_e, region - rows_e]).astype(I32)
    xb = _dispatch([(r[0], r[2], grp[4]) for r, grp in zip(routed, groups)], sizes_f, offs_f, gaps, n_used,
                   cap=cap, bm=bm)
    yb = _experts(xb, blk_e, blk_valid, blk_next_e, n_used, w_gu, b_gu, w_dn, b_dn, layer=layer, bm=bm)
    outs, first = [], 0
    for (x, _, _, gate, tm), (_, top_w, pos_t, _) in zip(groups, routed):
        outs.append(_combine(x, gate, pos_t.T, top_w.T, yb, sizes_f, offs_f, tm=tm, first_tile=first))
        first += x.shape[0] // tm
    return outs


def _hybrid_mixer(x, l, mod, pos, attend, sc_prev, p, *, b, s_len, prompt):
    t, d = x.shape
    tm = min(1024, s_len) if prompt else t
    sh1, sc1, g1 = mod
    li = l // 2
    lam_init = 0.8 - 0.6 * math.exp(-0.3 * l)
    qb, kr, kb, vr, vb, z = _in_proj(x, p["norm_mix_g"][l], sc1, sh1, p["hy_w_in"][li], _rope_tables(pos),
                                     p["q_norm_g"][li], p["k_norm_g"][li], tm=tm, tn=512)
    lams = (p["lambda_q1"][li], p["lambda_k1"][li], p["lambda_q2"][li], p["lambda_k2"][li])
    o = attend(qb, kb, vb, kr, vr, lams, p["subln_g"][li], li, lam_init)
    c_sc = p["sconv_w"].shape[-1]
    bcol, ccol, ucol = 0, 1, 2
    if prompt:
        o_sc, tail = _conv_prompt(z, ucol, sc_prev[li], p["sconv_w"][li], b=b, s_len=s_len, c=c_sc,
                                  tt=min(512, s_len), gate_cols=(ccol, bcol))
        sc_new = tail[:, SUBLANES - (SC_KERNEL - 1):]
    else:
        tmaj = lambda col: z[:, col * c_sc:(col + 1) * c_sc].reshape(b, s_len, c_sc).transpose(1, 0, 2)
        o_t, cu_t = _conv_sample(tmaj(ucol), sc_prev[li].transpose(1, 0, 2), p["sconv_w"][li],
                                 gates=(tmaj(ccol), tmaj(bcol)))
        o_sc = o_t.transpose(1, 0, 2).reshape(t, c_sc)
        sc_new = jnp.concatenate([sc_prev[li], cu_t.transpose(1, 0, 2)], axis=1)[:, -(SC_KERNEL - 1):]
    w_out = p["hy_w_out"][li]
    x = _matmul_residual([o, o_sc], [w_out[:d], w_out[d:]], None, x, g1, tm=tm, tn=512)
    n_heads = d // V_DIM
    return x, (kr.reshape(b, s_len, n_heads, V_DIM), vr.reshape(b, s_len, n_heads, V_DIM), sc_new)


def _conformer_mixer(x, l, mod, conf_prev, p, *, b, s_len, prompt):
    t, d = x.shape
    tm = min(1024, s_len) if prompt else t
    sh1, sc1, g1 = mod
    ci = l // 2
    u = _norm_mod_matmul(x, p["norm_mix_g"][l], sc1, sh1, p["conf_w1"][ci], p["conf_b1"][ci], tm=tm, tn=512, glu=True)
    ln_params = (p["conf_dw_b"][ci], p["conf_ln_g"][ci], p["conf_ln_b"][ci])
    if prompt:
        y = _conv_prompt(u, 0, conf_prev[ci], p["conf_dw"][ci], b=b, s_len=s_len, c=d, tt=min(512, s_len),
                         ln_params=ln_params)
        cf_new = u.reshape(b, s_len, d)[:, s_len - (CONF_KERNEL - 1):]
    else:
        u_t = u.reshape(b, s_len, d).transpose(1, 0, 2)
        y_t, _ = _conv_sample(u_t, conf_prev[ci].transpose(1, 0, 2), p["conf_dw"][ci], ln_params=ln_params)
        y = y_t.transpose(1, 0, 2).reshape(t, d)
        cf_new = jnp.concatenate([conf_prev[ci], u.reshape(b, s_len, d)], axis=1)[:, -(CONF_KERNEL - 1):]
    x = _matmul_residual([y], [p["conf_w2"][ci]], p["conf_b2"][ci], x, g1, tm=tm, tn=512)
    return x, (cf_new,)


def _mixer(x, l, mod, pos, attend, sc_prev, conf_prev, p, *, b, s_len, prompt):
    if l % 2 == 0:
        return _hybrid_mixer(x, l, mod, pos, attend, sc_prev, p, b=b, s_len=s_len, prompt=prompt)
    return _conformer_mixer(x, l, mod, conf_prev, p, b=b, s_len=s_len, prompt=prompt)


def kernel(x_prompt, x_sample, cache_k, cache_v, state_sconv, state_conf, page_table, c_prompt, c_sample,
           ada_w, ada_b, norm_mix_g, norm_ffn_g, hy_w_in, hy_w_out, q_norm_g, k_norm_g,
           lambda_q1, lambda_k1, lambda_q2, lambda_k2, subln_g, sconv_w,
           conf_w1, conf_b1, conf_dw, conf_dw_b, conf_ln_g, conf_ln_b, conf_w2, conf_b2,
           router_w, router_b, moe_w_gu, moe_b_gu, moe_w_dn, moe_b_dn):
    bp, s_len, d = x_prompt.shape
    bd, t_new, _ = x_sample.shape
    depth = ada_w.shape[0]
    p = {"ada_w": ada_w, "norm_mix_g": norm_mix_g, "norm_ffn_g": norm_ffn_g,
         "hy_w_in": hy_w_in.astype(BF16), "hy_w_out": hy_w_out.astype(BF16),
         "q_norm_g": q_norm_g, "k_norm_g": k_norm_g,
         "lambda_q1": lambda_q1, "lambda_k1": lambda_k1, "lambda_q2": lambda_q2, "lambda_k2": lambda_k2,
         "subln_g": subln_g, "sconv_w": sconv_w,
         "conf_w1": conf_w1.astype(BF16), "conf_b1": conf_b1, "conf_dw": conf_dw, "conf_dw_b": conf_dw_b,
         "conf_ln_g": conf_ln_g, "conf_ln_b": conf_ln_b, "conf_w2": conf_w2.astype(BF16), "conf_b2": conf_b2,
         "router_w_t": jnp.swapaxes(router_w, 1, 2), "router_b": router_b,
         "moe_w_gu": moe_w_gu, "moe_b_gu": moe_b_gu, "moe_w_dn": moe_w_dn, "moe_b_dn": moe_b_dn}

    mod_all = _ada_mod(jnp.concatenate([c_prompt, c_sample], axis=0), ada_w, ada_b)
    mods_p, mods_s = [], []
    for l in range(depth):
        parts = [mod_all[l, :, i * d:(i + 1) * d] for i in range(6)]
        mods_p.append([m[:bp].reshape(bp, 1, d) for m in parts])
        mods_s.append([jnp.repeat(m[bp:], t_new, axis=0).reshape(1, bd * t_new, d) for m in parts])

    past_len = page_table.shape[1] * PAGE_SIZE
    pos_prompt = jnp.arange(s_len, dtype=I32)
    pos_sample = jnp.tile(past_len + jnp.arange(t_new, dtype=I32), bd)

    def attend_prompt(qb, kb, vb, kr, vr, lams, sg, li, lam_init):
        return _flash_prompt(qb, kb, vb, lams, sg, b=bp, s_len=s_len, d=d, lam_init=lam_init,
                             tq=min(512, s_len), hb=4)

    def attend_sample(qb, kb, vb, kr, vr, lams, sg, li, lam_init):
        return _decode_attn(qb, kr, vr, cache_k, cache_v, li, page_table, lams, sg, bd=bd, t_new=t_new,
                            lam_init=lam_init, pages=math.gcd(DECODE_PAGES, page_table.shape[1]))

    n_even, n_odd = (depth + 1) // 2, depth // 2
    sc_zero = jnp.zeros((n_even, bp, SC_KERNEL - 1, sconv_w.shape[-1]), F32)
    conf_zero = jnp.zeros((n_odd, bp, CONF_KERNEL - 1, conf_dw.shape[-1]), F32)
    xp = x_prompt.reshape(bp * s_len, d)
    xs = x_sample.reshape(bd * t_new, d)
    new_p, new_s = [], []
    for l in range(depth):
        xp, st_p = _mixer(xp, l, mods_p[l][:3], pos_prompt, attend_prompt, sc_zero, conf_zero, p,
                          b=bp, s_len=s_len, prompt=True)
        xs, st_s = _mixer(xs, l, mods_s[l][:3], pos_sample, attend_sample, state_sconv, state_conf, p,
                          b=bd, s_len=t_new, prompt=False)
        new_p.append(st_p)
        new_s.append(st_s)
        (sh2p, sc2p, g2p), (sh2s, sc2s, g2s) = mods_p[l][3:], mods_s[l][3:]
        xp, xs = _moe_layer([(xp, sc2p, sh2p, g2p, min(256, bp * s_len)), (xs, sc2s, sh2s, g2s, bd * t_new)],
                            norm_ffn_g[l], p["router_w_t"][l], router_b[l],
                            moe_w_gu, moe_b_gu, moe_w_dn, moe_b_dn, layer=l, bm=MOE_BLOCK)

    def collect(new, b, s):
        even, odd = new[0::2], new[1::2]
        stack = lambda parts: parts[0][None] if len(parts) == 1 else jnp.stack(parts)
        return (stack([e[0] for e in even]), stack([e[1] for e in even]), stack([e[2] for e in even]),
                stack([o[0] for o in odd]))

    k_p, v_p, sc_p, cf_p = collect(new_p, bp, s_len)
    k_s, v_s, sc_s, cf_s = collect(new_s, bd, t_new)
    return (xp.reshape(bp, s_len, d), xs.reshape(bd, t_new, d), k_p, v_p, sc_p, cf_p, k_s, v_s, sc_s, cf_s)
```

```python
import functools
import math

import jax
import jax.numpy as jnp
from jax import lax
from jax.experimental import pallas as pl
from jax.experimental.pallas import tpu as pltpu

F32 = jnp.float32
BF16 = jnp.bfloat16
I32 = jnp.int32

EPS = 1e-6
HEAD_DIM = 64
V_DIM = 2 * HEAD_DIM
ROPE_DIM = HEAD_DIM // 4
ROPE_THETA = 500000.0
SC_KERNEL = 3
CONF_KERNEL = 31
N_EXPERTS = 32
TOP_K = 4
SWIGLU_LIMIT = 7.0
SWIGLU_ALPHA = 1.702
PAGE_SIZE = 128
DECODE_PAGES = 16
Q_SCALE = HEAD_DIM ** -0.5 * math.log2(math.e)

VMEM_LIMIT_BYTES = 56 * 1024 * 1024
SUBLANES = 8
MOE_ROW_ALIGN = 8
MOE_BLOCK = 512
MOE_SUB_BLOCK = 256


def _params(*sem):
    return pltpu.CompilerParams(dimension_semantics=sem, vmem_limit_bytes=VMEM_LIMIT_BYTES)


def _sigmoid(x):
    return 1.0 / (1.0 + jnp.exp(-x))


def _dot(a, b):
    return jnp.dot(a, b, preferred_element_type=F32)


def _dot_nt(a, b):
    return lax.dot_general(a, b, (((1,), (1,)), ((), ())), preferred_element_type=F32)


def _rmsnorm_mod(x, g, sc, sh):
    ms = jnp.mean(x * x, axis=-1, keepdims=True)
    return x * lax.rsqrt(ms + EPS) * g * (1.0 + sc) + sh


def _ada_kernel(c_ref, w_ref, b_ref, o_ref):
    c = c_ref[...]
    s = c * _sigmoid(c)
    o_ref[0] = jnp.dot(s, w_ref[0], precision=lax.Precision.HIGHEST, preferred_element_type=F32) + b_ref[0]


def _ada_mod(c_all, ada_w, ada_b):
    depth, d, n = ada_w.shape
    r = c_all.shape[0]
    tn = 1536
    return pl.pallas_call(
        _ada_kernel,
        out_shape=jax.ShapeDtypeStruct((depth, r, n), F32),
        grid=(depth, n // tn),
        in_specs=[pl.BlockSpec((r, d), lambda l, j: (0, 0)),
                  pl.BlockSpec((1, d, tn), lambda l, j: (l, 0, j)),
                  pl.BlockSpec((1, 1, tn), lambda l, j: (l, 0, j))],
        out_specs=pl.BlockSpec((1, r, tn), lambda l, j: (l, 0, j)),
        compiler_params=_params("arbitrary", "arbitrary"),
        name="ada_mod",
    )(c_all, ada_w, ada_b.reshape(depth, 1, n))


def _nmm_kernel(*refs, glu, has_bias):
    x_ref, g_ref, sc_ref, sh_ref = refs[:4]
    rest = list(refs[4:])
    w_refs = [rest.pop(0) for _ in range(2 if glu else 1)]
    b_refs = [rest.pop(0) for _ in range((2 if glu else 1) if has_bias else 0)]
    o_ref, h_scr = rest

    @pl.when(pl.program_id(1) == 0)
    def _():
        h = _rmsnorm_mod(x_ref[...], g_ref[...], sc_ref[0], sh_ref[0])
        h_scr[...] = h.astype(BF16)

    h = h_scr[...]
    accs = []
    for i, w_ref in enumerate(w_refs):
        a = _dot(h, w_ref[...])
        if has_bias:
            a = a + b_refs[i][...]
        accs.append(a)
    if glu:
        o_ref[...] = (accs[0] * _sigmoid(accs[1])).astype(o_ref.dtype)
    else:
        o_ref[...] = accs[0].astype(o_ref.dtype)


def _norm_mod_matmul(x, g, sc, sh, w, bias, *, tm, tn, glu=False, out_dtype=F32):
    t, d = x.shape
    n_out = w.shape[1] // 2 if glu else w.shape[1]
    ns, r, _ = sc.shape
    tps = (t // tm) // ns
    nj = n_out // tn
    in_specs = [pl.BlockSpec((tm, d), lambda i, j: (i, 0)),
                pl.BlockSpec((1, d), lambda i, j: (0, 0)),
                pl.BlockSpec((1, r, d), lambda i, j: (i // tps, 0, 0)),
                pl.BlockSpec((1, r, d), lambda i, j: (i // tps, 0, 0)),
                pl.BlockSpec((d, tn), lambda i, j: (0, j))]
    args = [x, g.reshape(1, d), sc, sh, w]
    if glu:
        in_specs.append(pl.BlockSpec((d, tn), lambda i, j: (0, j + nj)))
        args.append(w)
    if bias is not None:
        b2 = bias.reshape(1, -1)
        in_specs.append(pl.BlockSpec((1, tn), lambda i, j: (0, j)))
        args.append(b2)
        if glu:
            in_specs.append(pl.BlockSpec((1, tn), lambda i, j: (0, j + nj)))
            args.append(b2)
    return pl.pallas_call(
        functools.partial(_nmm_kernel, glu=glu, has_bias=bias is not None),
        out_shape=jax.ShapeDtypeStruct((t, n_out), out_dtype),
        grid=(t // tm, nj),
        in_specs=in_specs,
        out_specs=pl.BlockSpec((tm, tn), lambda i, j: (i, j)),
        scratch_shapes=[pltpu.VMEM((tm, d), BF16)],
        compiler_params=_params("arbitrary", "arbitrary"),
        name="norm_mod_matmul",
    )(*args)


def _group_mean_square(xh):
    r = lax.broadcasted_iota(I32, (V_DIM, V_DIM), 0) // HEAD_DIM
    c = lax.broadcasted_iota(I32, (V_DIM, V_DIM), 1) // HEAD_DIM
    ones_blk = jnp.where(r == c, 1.0, 0.0).astype(BF16)
    x2 = xh * xh
    hi = x2.astype(BF16)
    lo = (x2 - hi.astype(F32)).astype(BF16)
    return (_dot(hi, ones_blk) + _dot(lo, ones_blk)) * (1.0 / HEAD_DIM)


def _in_proj_kernel(x_ref, g_ref, sc_ref, sh_ref, w_ref, cos_ref, s1_ref, s2_ref, qg_ref, kg_ref,
                    qb_ref, krow_ref, kb_ref, vrow_ref, vb_ref, rest_ref, h_scr, *, steps_per_part):
    j = pl.program_id(1)
    spp = steps_per_part

    @pl.when(j == 0)
    def _():
        h_scr[...] = _rmsnorm_mod(x_ref[...], g_ref[...], sc_ref[0], sh_ref[0]).astype(BF16)

    acc = _dot(h_scr[...], w_ref[...])
    heads = acc.shape[1] // V_DIM

    def norm_rope(xh, g):
        y = xh * lax.rsqrt(_group_mean_square(xh) + EPS) * g
        return (y * cos_ref[...] + pltpu.roll(y, V_DIM - ROPE_DIM // 2, 1) * s1_ref[...]
                + pltpu.roll(y, ROPE_DIM // 2, 1) * s2_ref[...])

    @pl.when(j < spp)
    def _():
        for h in range(heads):
            sl = slice(h * V_DIM, (h + 1) * V_DIM)
            qb_ref[:, sl] = (norm_rope(acc[:, sl], qg_ref[...]) * Q_SCALE).astype(BF16)

    @pl.when((j >= spp) & (j < 2 * spp))
    def _():
        for h in range(heads):
            sl = slice(h * V_DIM, (h + 1) * V_DIM)
            kh = norm_rope(acc[:, sl], kg_ref[...])
            krow_ref[:, sl] = kh
            kb_ref[:, sl] = kh.astype(BF16)

    @pl.when((j >= 2 * spp) & (j < 3 * spp))
    def _():
        vrow_ref[...] = acc
        vb_ref[...] = acc.astype(BF16)

    @pl.when(j >= 3 * spp)
    def _():
        rest_ref[...] = acc


def _rope_tables(pos):
    half = ROPE_DIM // 2
    inv_freq = jnp.exp(-math.log(ROPE_THETA) * jnp.arange(half, dtype=F32) / half)
    ang = pos.astype(F32)[:, None] * inv_freq[None, :]
    cos, sin = jnp.cos(ang), jnp.sin(ang)
    p = pos.shape[0]
    ones = jnp.ones((p, HEAD_DIM - ROPE_DIM), F32)
    zeros = jnp.zeros((p, HEAD_DIM - ROPE_DIM), F32)
    z8 = jnp.zeros((p, half), F32)
    c64 = jnp.concatenate([cos, cos, ones], axis=1)
    s1_64 = jnp.concatenate([-sin, z8, zeros], axis=1)
    s2_64 = jnp.concatenate([z8, sin, zeros], axis=1)
    two = lambda a: jnp.concatenate([a, a], axis=1)
    return two(c64), two(s1_64), two(s2_64)


def _in_proj(x, g, sc, sh, w, tables, q_g, k_g, *, tm, tn):
    t, d = x.shape
    n_rest = w.shape[1] - 3 * d
    spp = d // tn
    nj = w.shape[1] // tn
    ns, r, _ = sc.shape
    tps = (t // tm) // ns
    npb = tables[0].shape[0] // tm
    part = lambda k: pl.BlockSpec((tm, tn), lambda i, j: (i, jnp.clip(j - k * spp, 0, spp - 1)))
    tab = pl.BlockSpec((tm, V_DIM), lambda i, j: (i % npb, 0))
    gspec = pl.BlockSpec((1, V_DIM), lambda i, j: (0, 0))
    mod = pl.BlockSpec((1, r, d), lambda i, j: (i // tps, 0, 0))
    g2 = lambda v: jnp.concatenate([v, v]).reshape(1, V_DIM)
    return pl.pallas_call(
        functools.partial(_in_proj_kernel, steps_per_part=spp),
        out_shape=[jax.ShapeDtypeStruct((t, d), BF16), jax.ShapeDtypeStruct((t, d), F32),
                   jax.ShapeDtypeStruct((t, d), BF16), jax.ShapeDtypeStruct((t, d), F32),
                   jax.ShapeDtypeStruct((t, d), BF16), jax.ShapeDtypeStruct((t, n_rest), F32)],
        grid=(t // tm, nj),
        in_specs=[pl.BlockSpec((tm, d), lambda i, j: (i, 0)), pl.BlockSpec((1, d), lambda i, j: (0, 0)), mod, mod,
                  pl.BlockSpec((d, tn), lambda i, j: (0, j)), tab, tab, tab, gspec, gspec],
        out_specs=[part(0), part(1), part(1), part(2), part(2),
                   pl.BlockSpec((tm, tn), lambda i, j: (i, jnp.clip(j - 3 * spp, 0, n_rest // tn - 1)))],
        scratch_shapes=[pltpu.VMEM((tm, d), BF16)],
        compiler_params=_params("arbitrary", "arbitrary"),
        name="in_proj",
    )(x, g.reshape(1, d), sc, sh, w, *tables, g2(q_g), g2(k_g))


def _diff_lambda(lq1, lk1, lq2, lk2, lam_init):
    a = jnp.sum(lq1 * lk1, axis=-1, keepdims=True)
    b = jnp.sum(lq2 * lk2, axis=-1, keepdims=True)
    return jnp.exp(a) - jnp.exp(b) + lam_init


def _subln(o, g, lam_init):
    ms = jnp.mean(o * o, axis=-1, keepdims=True)
    return o * lax.rsqrt(ms + EPS) * g * (1.0 - lam_init)


def _flash_kernel(q_ref, k_ref, v_ref, lq1, lk1, lq2, lk2, sg_ref, o_ref,
                  qm_scr, m_scr, acc_scr, *, lam_init, tq, hb):
    qi, ki, nk = pl.program_id(2), pl.program_id(3), pl.num_programs(3)

    @pl.when(ki == 0)
    def _():
        for h in range(hb):
            q = q_ref[:, h * V_DIM:(h + 1) * V_DIM]
            lane = lax.broadcasted_iota(I32, q.shape, 1)
            zero = jnp.zeros_like(q)
            qm_scr[h, pl.ds(0, tq), :] = jnp.where(lane < HEAD_DIM, q, zero)
            qm_scr[h, pl.ds(tq, tq), :] = jnp.where(lane >= HEAD_DIM, q, zero)
        m_scr[...] = jnp.full(m_scr.shape, -jnp.inf, F32)
        acc_scr[...] = jnp.zeros(acc_scr.shape, F32)

    def update(h, r0, nr, c0, nc, causal):
        rows, lanes = pl.ds(r0, nr), slice(h * V_DIM, (h + 1) * V_DIM)
        k = k_ref[pl.ds(c0, nc), lanes]
        v = v_ref[pl.ds(c0, nc), lanes]
        v_ext = jnp.concatenate([v, jnp.ones_like(v)], axis=1)
        s = _dot_nt(qm_scr[h, rows, :], k)
        if causal:
            row = (lax.broadcasted_iota(I32, s.shape, 0) + r0) % tq
            col = lax.broadcasted_iota(I32, s.shape, 1) + c0
            s = jnp.where(col <= row, s, -jnp.inf)
        m_old = m_scr[h, rows, :]
        m_new = jnp.maximum(m_old, jnp.max(s, axis=-1, keepdims=True))
        alpha = jnp.exp2(m_old - m_new)
        p = jnp.concatenate([jnp.exp2(s[:, j * V_DIM:(j + 1) * V_DIM] - m_new) for j in range(nc // V_DIM)], axis=1)
        acc_scr[h, rows, :] = (jnp.concatenate([alpha, alpha], axis=1) * acc_scr[h, rows, :]
                               + _dot(p.astype(BF16), v_ext))
        m_scr[h, rows, :] = m_new

    @pl.when(ki < qi)
    def _():
        for h in range(hb):
            update(h, 0, 2 * tq, 0, tq, False)

    @pl.when(ki == qi)
    def _():
        half = tq // 2
        for h in range(hb):
            update(h, 0, 2 * tq, 0, half, True)
            for c in range(2):
                update(h, c * tq + half, half, half, half, True)

    @pl.when(ki == nk - 1)
    def _():
        lam = _diff_lambda(lq1[...], lk1[...], lq2[...], lk2[...], lam_init)
        for h in range(hb):
            acc = acc_scr[h]
            o = acc[:tq, :V_DIM] / acc[:tq, V_DIM:] - lam * (acc[tq:, :V_DIM] / acc[tq:, V_DIM:])
            o_ref[:, h * V_DIM:(h + 1) * V_DIM] = _subln(o, sg_ref[...], lam_init).astype(o_ref.dtype)


def _flash_prompt(qb, kb, vb, lams, subln_g, *, b, s_len, d, lam_init, tq, hb):
    n_heads = d // V_DIM
    nq = s_len // tq
    w = hb * V_DIM
    lam_spec = pl.BlockSpec((1, HEAD_DIM), lambda bi, h, qi, ki: (0, 0))
    return pl.pallas_call(
        functools.partial(_flash_kernel, lam_init=lam_init, tq=tq, hb=hb),
        out_shape=jax.ShapeDtypeStruct((b * s_len, d), BF16),
        grid=(b, n_heads // hb, nq, nq),
        in_specs=[pl.BlockSpec((tq, w), lambda bi, h, qi, ki: (bi * nq + qi, h)),
                  pl.BlockSpec((tq, w), lambda bi, h, qi, ki: (bi * nq + jnp.minimum(ki, qi), h)),
                  pl.BlockSpec((tq, w), lambda bi, h, qi, ki: (bi * nq + jnp.minimum(ki, qi), h)),
                  lam_spec, lam_spec, lam_spec, lam_spec,
                  pl.BlockSpec((1, V_DIM), lambda bi, h, qi, ki: (0, 0))],
        out_specs=pl.BlockSpec((tq, w), lambda bi, h, qi, ki: (bi * nq + qi, h)),
        scratch_shapes=[pltpu.VMEM((hb, 2 * tq, V_DIM), BF16), pltpu.VMEM((hb, 2 * tq, V_DIM), F32),
                        pltpu.VMEM((hb, 2 * tq, 2 * V_DIM), F32)],
        compiler_params=_params("arbitrary", "arbitrary", "arbitrary", "arbitrary"),
        name="flash_prompt",
    )(qb, kb, vb, *[x.reshape(1, HEAD_DIM) for x in lams], subln_g.reshape(1, V_DIM))


def _decode_kernel(pt_ref, q_ref, kn_ref, vn_ref, lq1, lk1, lq2, lk2, sg_ref, *refs,
                   lam_init, n_heads, pages, t_new, page_base):
    ck_hbm, cv_hbm, o_ref, kbuf, vbuf, sems, q_scr, m_scr, l_scr, acc_scr = refs
    bi, ci, nc = pl.program_id(0), pl.program_id(1), pl.num_programs(1)
    step, n_steps = bi * nc + ci, pl.num_programs(0) * nc
    rows = 2 * t_new * n_heads
    rows_per_page = PAGE_SIZE * n_heads
    row_head = lax.broadcasted_iota(I32, (rows, V_DIM), 0) // (2 * t_new)
    lane = lax.broadcasted_iota(I32, (rows, V_DIM), 1)

    def page_copies(g, slot):
        b, c = g // nc, g % nc
        copies = []
        for p in range(pages):
            page = page_base + pt_ref[b, c * pages + p]
            dst = pl.ds(p * rows_per_page, rows_per_page)
            copies.append(pltpu.make_async_copy(ck_hbm.at[page], kbuf.at[slot, dst], sems.at[slot, 0]))
            copies.append(pltpu.make_async_copy(cv_hbm.at[page], vbuf.at[slot, dst], sems.at[slot, 1]))
        return copies

    slot = step % 2

    @pl.when(step == 0)
    def _():
        for cp in page_copies(step, slot):
            cp.start()

    @pl.when(step + 1 < n_steps)
    def _():
        for cp in page_copies(step + 1, 1 - slot):
            cp.start()

    @pl.when(ci == 0)
    def _():
        comp = (lax.broadcasted_iota(I32, (rows, V_DIM), 0) // t_new) % 2
        q = q_ref[0]
        q_scr[...] = jnp.where(lane // HEAD_DIM == comp, q, jnp.zeros_like(q))
        n_new = t_new * n_heads
        s = _dot_nt(q_scr[...], kn_ref[0].astype(BF16))
        r = lax.broadcasted_iota(I32, s.shape, 0)
        c = lax.broadcasted_iota(I32, s.shape, 1)
        visible = (c % n_heads == r // (2 * t_new)) & (c // n_heads <= r % t_new)
        s = jnp.where(visible, s, -jnp.inf)
        m = jnp.max(s, axis=-1, keepdims=True)
        p = jnp.exp2(s - m)
        m_scr[...] = jnp.broadcast_to(m, m_scr.shape)
        l_scr[...] = jnp.broadcast_to(jnp.sum(p, axis=-1, keepdims=True), l_scr.shape)
        acc_scr[...] = _dot(p.astype(BF16), vn_ref[0].astype(BF16))

    for cp in page_copies(step, slot):
        cp.wait()
    k2 = kbuf[slot].astype(BF16)
    v2 = vbuf[slot].astype(BF16)
    s = _dot_nt(q_scr[...], k2)
    bias = jnp.where(lane % n_heads == row_head, 0.0, -jnp.inf)
    chunks = [s[:, j * V_DIM:(j + 1) * V_DIM] + bias for j in range(s.shape[1] // V_DIM)]
    m_old = m_scr[...]
    m_new = jnp.maximum(m_old, jnp.max(functools.reduce(jnp.maximum, chunks), axis=-1, keepdims=True))
    alpha = jnp.exp2(m_old - m_new)
    ps = [jnp.exp2(ch - m_new) for ch in chunks]
    l_scr[...] = alpha * l_scr[...] + jnp.sum(functools.reduce(jnp.add, ps), axis=-1, keepdims=True)
    acc_scr[...] = alpha * acc_scr[...] + _dot(jnp.concatenate(ps, axis=1).astype(BF16), v2)
    m_scr[...] = m_new

    @pl.when(ci == nc - 1)
    def _():
        lam = _diff_lambda(lq1[...], lk1[...], lq2[...], lk2[...], lam_init)
        o = acc_scr[...] / l_scr[...]
        o = o - lam * pltpu.roll(o, rows - t_new, 0)
        o_ref[0] = _subln(o, sg_ref[...], lam_init)


def _decode_attn(qb, k_new, v_new, cache_k, cache_v, li, page_table, lams, subln_g, *, bd, t_new, lam_init, pages):
    d = qb.shape[1]
    n_heads = d // V_DIM
    n_pool = cache_k.shape[1]
    rows_per_page = PAGE_SIZE * n_heads
    ck = cache_k.reshape(-1, rows_per_page, V_DIM)
    cv = cache_v.reshape(-1, rows_per_page, V_DIM)
    n_pages = page_table.shape[1]
    nc = n_pages // pages
    rows = 2 * t_new * n_heads
    n_new = t_new * n_heads
    q2 = jnp.broadcast_to(qb.reshape(bd, t_new, n_heads, 1, V_DIM).transpose(0, 2, 3, 1, 4),
                          (bd, n_heads, 2, t_new, V_DIM)).reshape(bd, rows, V_DIM)

    new_spec = pl.BlockSpec((1, n_new, V_DIM), lambda b, c, pt: (b, 0, 0))
    lam_spec = pl.BlockSpec((1, HEAD_DIM), lambda b, c, pt: (0, 0))
    chunk_rows = pages * rows_per_page
    grid_spec = pltpu.PrefetchScalarGridSpec(
        num_scalar_prefetch=1,
        grid=(bd, nc),
        in_specs=[pl.BlockSpec((1, rows, V_DIM), lambda b, c, pt: (b, 0, 0)), new_spec, new_spec,
                  lam_spec, lam_spec, lam_spec, lam_spec, pl.BlockSpec((1, V_DIM), lambda b, c, pt: (0, 0)),
                  pl.BlockSpec(memory_space=pl.ANY), pl.BlockSpec(memory_space=pl.ANY)],
        out_specs=pl.BlockSpec((1, rows, V_DIM), lambda b, c, pt: (b, 0, 0)),
        scratch_shapes=[pltpu.VMEM((2, chunk_rows, V_DIM), F32), pltpu.VMEM((2, chunk_rows, V_DIM), F32),
                        pltpu.SemaphoreType.DMA((2, 2)),
                        pltpu.VMEM((rows, V_DIM), BF16), pltpu.VMEM((rows, V_DIM), F32),
                        pltpu.VMEM((rows, V_DIM), F32), pltpu.VMEM((rows, V_DIM), F32)])
    o2 = pl.pallas_call(
        functools.partial(_decode_kernel, lam_init=lam_init, n_heads=n_heads, pages=pages, t_new=t_new,
                          page_base=li * n_pool),
        out_shape=jax.ShapeDtypeStruct((bd, rows, V_DIM), F32),
        grid_spec=grid_spec,
        compiler_params=_params("arbitrary", "arbitrary"),
        name="decode_attn",
    )(page_table, q2, k_new.reshape(bd, n_new, V_DIM), v_new.reshape(bd, n_new, V_DIM),
      *[x.reshape(1, HEAD_DIM) for x in lams], subln_g.reshape(1, V_DIM), ck, cv)
    o = o2.reshape(bd, n_heads, 2, t_new, V_DIM)[:, :, 0].transpose(0, 2, 1, 3)
    return o.reshape(bd * t_new, d).astype(BF16)


def _layernorm_silu(y, g, b):
    mu = jnp.mean(y, axis=-1, keepdims=True)
    yc = y - mu
    var = jnp.mean(yc * yc, axis=-1, keepdims=True)
    z = yc * lax.rsqrt(var + EPS) * g + b
    return z * _sigmoid(z)


CONV_ROW_CHUNK = 128
CONV_LANE_CHUNK = 128


def _depthwise_taps(ext, w_ref, y_scr, *, taps, base, tt):
    c = y_scr.shape[1]
    rc, cc = CONV_ROW_CHUNK, min(CONV_LANE_CHUNK, c)
    by_residue = {}
    for j in range(taps):
        by_residue.setdefault((base + j) % SUBLANES, []).append(j)

    def row_chunk(i, carry):
        row0 = pl.multiple_of(i * rc, rc)
        for c0 in range(0, c, cc):
            y = None
            for r, js in sorted(by_residue.items()):
                acc = None
                for j in js:
                    aligned = base + j - r
                    term = (w_ref[pl.ds(j, 1), pl.ds(c0, cc)]
                            * ext[pl.ds(row0 + aligned, rc + SUBLANES), pl.ds(c0, cc)])
                    acc = term if acc is None else acc + term
                part = acc[r:r + rc]
                y = part if y is None else y + part
            y_scr[pl.ds(row0, rc), pl.ds(c0, cc)] = y
        return carry

    lax.fori_loop(0, tt // rc, row_chunk, 0)


def _conv_prompt_kernel(*refs, taps, halo, tt, gated, ln):
    refs = list(refs)
    u_ref, uh_ref, prev_ref = refs[:3]
    del refs[:3]
    if gated:
        gc_ref, gch_ref, gb_ref = refs[:3]
        del refs[:3]
    w_ref = refs.pop(0)
    if ln:
        cb_ref, lg_ref, lb_ref = refs[:3]
        del refs[:3]
    o_ref = refs.pop(0)
    if gated:
        tail_ref = refs.pop(0)
    ext, y_scr = refs
    ti, nt = pl.program_id(1), pl.num_programs(1)

    cur = u_ref[...]
    if gated:
        cur = cur * gc_ref[...]
    ext[pl.ds(halo, tt), :] = cur

    @pl.when(ti == 0)
    def _():
        ext[pl.ds(0, halo), :] = prev_ref[0]

    @pl.when(ti > 0)
    def _():
        hv = uh_ref[...]
        if gated:
            hv = hv * gch_ref[...]
        ext[pl.ds(0, halo), :] = hv

    ext[pl.ds(halo + tt, SUBLANES), :] = jnp.zeros((SUBLANES, ext.shape[1]), F32)
    _depthwise_taps(ext, w_ref, y_scr, taps=taps, base=halo - (taps - 1), tt=tt)
    y = y_scr[...]
    if ln:
        y = _layernorm_silu(y + cb_ref[...], lg_ref[...], lb_ref[...])
    if gated:
        y = gb_ref[...] * y

        @pl.when(ti == nt - 1)
        def _():
            tail_ref[0] = ext[pl.ds(halo + tt - SUBLANES, SUBLANES), :]
    o_ref[...] = y.astype(o_ref.dtype)


def _conv_prompt(src, ucol, prev, w, *, b, s_len, c, tt, gate_cols=None, ln_params=None):
    taps = w.shape[0]
    halo = SUBLANES if taps - 1 <= SUBLANES else 32
    nt = s_len // tt
    hpt = tt // halo
    prev_p = jnp.pad(prev, ((0, 0), (halo - (taps - 1), 0), (0, 0)))
    cur = lambda col: pl.BlockSpec((tt, c), lambda bi, ti: (bi * nt + ti, col))
    hal = lambda col: pl.BlockSpec((halo, c), lambda bi, ti: (jnp.maximum((bi * nt + ti) * hpt - 1, 0), col))
    vec = pl.BlockSpec((1, c), lambda bi, ti: (0, 0))
    in_specs = [cur(ucol), hal(ucol), pl.BlockSpec((1, halo, c), lambda bi, ti: (bi, 0, 0))]
    args = [src, src, prev_p]
    gated = gate_cols is not None
    if gated:
        cc, cb = gate_cols
        in_specs += [cur(cc), hal(cc), cur(cb)]
        args += [src, src, src]
    in_specs.append(pl.BlockSpec((taps, c), lambda bi, ti: (0, 0)))
    args.append(w)
    if ln_params is not None:
        in_specs += [vec, vec, vec]
        args += [p.reshape(1, c) for p in ln_params]
    out_shape = [jax.ShapeDtypeStruct((b * s_len, c), BF16)]
    out_specs = [pl.BlockSpec((tt, c), lambda bi, ti: (bi * nt + ti, 0))]
    if gated:
        out_shape.append(jax.ShapeDtypeStruct((b, SUBLANES, c), F32))
        out_specs.append(pl.BlockSpec((1, SUBLANES, c), lambda bi, ti: (bi, 0, 0)))
    res = pl.pallas_call(
        functools.partial(_conv_prompt_kernel, taps=taps, halo=halo, tt=tt, gated=gated, ln=ln_params is not None),
        out_shape=out_shape,
        grid=(b, nt),
        in_specs=in_specs,
        out_specs=out_specs,
        scratch_shapes=[pltpu.VMEM((halo + tt + SUBLANES, c), F32), pltpu.VMEM((tt, c), F32)],
        compiler_params=_params("arbitrary", "arbitrary"),
        name="conv_prompt",
    )(*args)
    return res if gated else res[0]


def _conv_sample_kernel(*refs, taps, s_new, gated, ln):
    refs = list(refs)
    prev_ref, u_ref = refs[:2]
    del refs[:2]
    if gated:
        gc_ref, gb_ref = refs[:2]
        del refs[:2]
    w_ref = refs.pop(0)
    if ln:
        cb_ref, lg_ref, lb_ref = refs[:3]
        del refs[:3]
    o_ref, cu_ref, ext = refs
    for j in range(taps - 1):
        ext[j] = prev_ref[j]
    for s in range(s_new):
        cu = u_ref[s]
        if gated:
            cu = cu * gc_ref[s]
        cu_ref[s] = cu
        ext[taps - 1 + s] = cu
    for s in range(s_new):
        y = w_ref[0] * ext[s]
        for j in range(1, taps):
            y = y + w_ref[j] * ext[s + j]
        if ln:
            y = _layernorm_silu(y + cb_ref[...], lg_ref[...], lb_ref[...])
        if gated:
            y = gb_ref[s] * y
        o_ref[s] = y.astype(o_ref.dtype)


def _conv_sample(u, prev, w, *, gates=None, ln_params=None):
    s_new, bd, c = u.shape
    taps = w.shape[0]
    args = [prev, u]
    gated = gates is not None
    if gated:
        args += list(gates)
    args.append(w.reshape(taps, 1, c))
    if ln_params is not None:
        args += [p.reshape(1, c) for p in ln_params]
    return pl.pallas_call(
        functools.partial(_conv_sample_kernel, taps=taps, s_new=s_new, gated=gated, ln=ln_params is not None),
        out_shape=[jax.ShapeDtypeStruct((s_new, bd, c), BF16), jax.ShapeDtypeStruct((s_new, bd, c), F32)],
        scratch_shapes=[pltpu.VMEM((taps - 1 + s_new, bd, c), F32)],
        compiler_params=pltpu.CompilerParams(vmem_limit_bytes=VMEM_LIMIT_BYTES),
        name="conv_sample",
    )(*args)


def _mm_res_kernel(*refs, n_in, has_bias):
    a_refs, w_refs = refs[:n_in], refs[n_in:2 * n_in]
    rest = list(refs[2 * n_in:])
    b_ref = rest.pop(0) if has_bias else None
    x_ref, gate_ref, o_ref = rest
    acc = _dot(a_refs[0][...], w_refs[0][...])
    for a_ref, w_ref in zip(a_refs[1:], w_refs[1:]):
        acc = acc + _dot(a_ref[...], w_ref[...])
    if has_bias:
        acc = acc + b_ref[...]
    o_ref[...] = x_ref[...] + gate_ref[0] * acc


def _matmul_residual(a_list, w_list, bias, x, gate, *, tm, tn):
    t, d = x.shape
    ns, r, _ = gate.shape
    tps = (t // tm) // ns
    n_in = len(a_list)
    in_specs = [pl.BlockSpec((tm, a.shape[1]), lambda i, j: (i, 0)) for a in a_list]
    in_specs += [pl.BlockSpec((w.shape[0], tn), lambda i, j: (0, j)) for w in w_list]
    args = list(a_list) + list(w_list)
    if bias is not None:
        in_specs.append(pl.BlockSpec((1, tn), lambda i, j: (0, j)))
        args.append(bias.reshape(1, d))
    in_specs += [pl.BlockSpec((tm, tn), lambda i, j: (i, j)),
                 pl.BlockSpec((1, r, tn), lambda i, j: (i // tps, 0, j))]
    args += [x, gate]
    return pl.pallas_call(
        functools.partial(_mm_res_kernel, n_in=n_in, has_bias=bias is not None),
        out_shape=jax.ShapeDtypeStruct((t, d), F32),
        grid=(t // tm, d // tn),
        in_specs=in_specs,
        out_specs=pl.BlockSpec((tm, tn), lambda i, j: (i, j)),
        compiler_params=_params("arbitrary", "arbitrary"),
        name="matmul_residual",
    )(*args)


def _router_kernel(x_ref, g_ref, sc_ref, sh_ref, rw_ref, rb_ref,
                   hb_ref, tw_ref, pos_ref, sizes_ref, *, tm):
    h = _rmsnorm_mod(x_ref[...], g_ref[...], sc_ref[0], sh_ref[0])
    hb_ref[...] = h.astype(BF16)
    logits = lax.dot_general(rw_ref[...], h, (((1,), (1,)), ((), ())),
                             precision=lax.Precision.HIGHEST, preferred_element_type=F32) + rb_ref[...]
    e_iota = lax.broadcasted_iota(I32, logits.shape, 0)
    work = logits
    ids, vals = [], []
    multi_hot = jnp.zeros(logits.shape, F32)
    for _ in range(TOP_K):
        m = jnp.max(work, axis=0, keepdims=True)
        idx = jnp.min(jnp.where(work == m, e_iota, N_EXPERTS), axis=0, keepdims=True)
        sel = e_iota == idx
        work = jnp.where(sel, -jnp.inf, work)
        multi_hot = multi_hot + jnp.where(sel, 1.0, 0.0)
        ids.append(idx)
        vals.append(m)
    ex = [jnp.exp(v - vals[0]) for v in vals]
    den = ex[0] + ex[1] + ex[2] + ex[3]
    r = lax.broadcasted_iota(I32, (tm, tm), 0)
    c = lax.broadcasted_iota(I32, (tm, tm), 1)
    before = jnp.where(r < c, 1.0, 0.0).astype(BF16)
    earlier = _dot(multi_hot.astype(BF16), before)
    cnt = jnp.sum(multi_hot, axis=1, keepdims=True)
    sizes = jnp.ceil(cnt * (1.0 / MOE_ROW_ALIGN)) * MOE_ROW_ALIGN
    er = lax.broadcasted_iota(I32, (N_EXPERTS, N_EXPERTS), 0)
    ec = lax.broadcasted_iota(I32, (N_EXPERTS, N_EXPERTS), 1)
    lower = jnp.where(ec < er, 1.0, 0.0).astype(BF16)
    run_start = _dot(lower, jnp.broadcast_to(sizes, (N_EXPERTS, V_DIM)).astype(BF16))[:, :1]
    row_in_tile = run_start + earlier
    for k in range(TOP_K):
        tw_ref[pl.ds(k, 1), :] = ex[k] / den
        pos_ref[pl.ds(k, 1), :] = jnp.sum(jnp.where(e_iota == ids[k], row_in_tile, 0.0),
                                          axis=0, keepdims=True).astype(I32)
    sizes_ref[0] = sizes.astype(I32)


def _router(x, g, sc, sh, rw_t, rb, *, tm):
    t, d = x.shape
    ns, r, _ = sc.shape
    nt = t // tm
    tps = nt // ns
    mod = pl.BlockSpec((1, r if r == 1 else tm, d), lambda i: (i // tps, 0 if r == 1 else i % tps, 0))
    lane4 = pl.BlockSpec((TOP_K, tm), lambda i: (0, i))
    return pl.pallas_call(
        functools.partial(_router_kernel, tm=tm),
        out_shape=[jax.ShapeDtypeStruct((t, d), BF16), jax.ShapeDtypeStruct((TOP_K, t), F32),
                   jax.ShapeDtypeStruct((TOP_K, t), I32), jax.ShapeDtypeStruct((nt, N_EXPERTS, 1), I32)],
        grid=(nt,),
        in_specs=[pl.BlockSpec((tm, d), lambda i: (i, 0)), pl.BlockSpec((1, d), lambda i: (0, 0)), mod, mod,
                  pl.BlockSpec((N_EXPERTS, d), lambda i: (0, 0)), pl.BlockSpec((N_EXPERTS, 1), lambda i: (0, 0))],
        out_specs=[pl.BlockSpec((tm, d), lambda i: (i, 0)), lane4, lane4,
                   pl.BlockSpec((1, N_EXPERTS, 1), lambda i: (i, 0, 0))],
        compiler_params=_params("arbitrary"),
        name="moe_router",
    )(x, g.reshape(1, d), sc, sh, rw_t, rb.reshape(N_EXPERTS, 1))


def _for_each_run(sizes_ref, offs_ref, tile, action):
    def one(e, local):
        n = pl.multiple_of(sizes_ref[tile * N_EXPERTS + e], MOE_ROW_ALIGN)
        off = pl.multiple_of(offs_ref[tile * N_EXPERTS + e], MOE_ROW_ALIGN)
        loc = pl.multiple_of(local, MOE_ROW_ALIGN)

        @pl.when(n > 0)
        def _():
            action(loc, off, n)
        return local + n

    lax.fori_loop(0, N_EXPERTS, one, 0)


def _zero_fill_unused(gap_ref, nu_ref, xb_hbm, zero_scr, sem, *, bm, nb):
    zero_scr[...] = jnp.zeros(zero_scr.shape, F32)

    def gap_copy(e):
        n = pl.multiple_of(gap_ref[N_EXPERTS + e], MOE_ROW_ALIGN)
        off = pl.multiple_of(gap_ref[e], MOE_ROW_ALIGN)
        return n, pltpu.make_async_copy(zero_scr.at[pl.ds(0, n)], xb_hbm.at[pl.ds(off, n)], sem)

    def blk_copy(b):
        return pltpu.make_async_copy(zero_scr, xb_hbm.at[pl.ds(pl.multiple_of(b * bm, bm), bm)], sem)

    def each_gap(action):
        def body(e, carry):
            n, cp = gap_copy(e)

            @pl.when(n > 0)
            def _():
                action(cp)
            return carry
        lax.fori_loop(0, N_EXPERTS, body, 0)

    def each_blk(action):
        def body(b, carry):
            action(blk_copy(b))
            return carry
        lax.fori_loop(nu_ref[0], nb, body, 0)

    each_gap(lambda cp: cp.start())
    each_blk(lambda cp: cp.start())
    each_gap(lambda cp: cp.wait())
    each_blk(lambda cp: cp.wait())


def _moe_rows(tm):
    return tm * TOP_K + N_EXPERTS * MOE_ROW_ALIGN


def _dispatch_kernel(sizes_ref, offs_ref, gap_ref, nu_ref, *refs, tile_ranges, bm, nb):
    n_groups = len(tile_ranges)
    hb_refs, pos_refs = refs[0:2 * n_groups:2], refs[1:2 * n_groups:2]
    xb_hbm, sorted_scr, zero_scr, sems = refs[2 * n_groups:]
    tile, n_tiles = pl.program_id(0), pl.num_programs(0)
    slot = tile % 2

    for hb_ref, pos_ref, (first, last) in zip(hb_refs, pos_refs, tile_ranges):
        @pl.when((tile >= first) & (tile < last))
        def _():
            tm = hb_ref.shape[0]
            rows = _moe_rows(tm)
            r_iota = lax.broadcasted_iota(I32, (rows, tm), 0)
            hit = r_iota == pos_ref[pl.ds(0, 1), :]
            for k in range(1, TOP_K):
                hit = hit | (r_iota == pos_ref[pl.ds(k, 1), :])
            perm = jnp.where(hit, 1.0, 0.0).astype(BF16)
            sorted_scr[slot, pl.ds(0, rows), :] = _dot(perm, hb_ref[...])

    def run_copy(s):
        def make(loc, off, n):
            return pltpu.make_async_copy(sorted_scr.at[s, pl.ds(loc, n)], xb_hbm.at[pl.ds(off, n)], sems.at[s])
        return make

    _for_each_run(sizes_ref, offs_ref, tile, lambda *a: run_copy(slot)(*a).start())

    @pl.when(tile > 0)
    def _():
        _for_each_run(sizes_ref, offs_ref, tile - 1, lambda *a: run_copy(1 - slot)(*a).wait())

    @pl.when(tile == n_tiles - 1)
    def _():
        _for_each_run(sizes_ref, offs_ref, tile, lambda *a: run_copy(slot)(*a).wait())
        _zero_fill_unused(gap_ref, nu_ref, xb_hbm, zero_scr, sems.at[0], bm=bm, nb=nb)


def _dispatch(groups, sizes, offs, gaps, n_used, *, cap, bm):
    d = groups[0][0].shape[1]
    in_specs, args, tile_ranges = [], [], []
    first = 0
    for hb, pos_t, tm in groups:
        nt = hb.shape[0] // tm
        local = lambda i, first=first, nt=nt: jnp.clip(i - first, 0, nt - 1)
        in_specs += [pl.BlockSpec((tm, d), lambda i, *_, local=local: (local(i), 0)),
                     pl.BlockSpec((TOP_K, tm), lambda i, *_, local=local: (0, local(i)))]
        args += [hb, pos_t]
        tile_ranges.append((first, first + nt))
        first += nt
    rows = max(_moe_rows(tm) for _, _, tm in groups)
    grid_spec = pltpu.PrefetchScalarGridSpec(
        num_scalar_prefetch=4,
        grid=(first,),
        in_specs=in_specs,
        out_specs=pl.BlockSpec(memory_space=pl.ANY),
        scratch_shapes=[pltpu.VMEM((2, rows, d), F32), pltpu.VMEM((bm, d), F32), pltpu.SemaphoreType.DMA((2,))])
    return pl.pallas_call(
        functools.partial(_dispatch_kernel, tile_ranges=tuple(tile_ranges), bm=bm, nb=cap // bm),
        out_shape=jax.ShapeDtypeStruct((cap, d), F32),
        grid_spec=grid_spec,
        compiler_params=_params("arbitrary"),
        name="moe_dispatch",
    )(sizes, offs, gaps, n_used, *args)


def _expert_kernel(be_ref, bv_ref, nx_ref, nu_ref, x_ref, bgu_ref, bdn_ref, wgu_hbm, wdn_hbm, y_ref,
                   stage_gu, stage_dn, wgu_scr, wdn_scr, sems, *, d_ff, sub, layer):
    b = pl.program_id(0)
    e = be_ref[b]
    prev_e = be_ref[jnp.maximum(b - 1, 0)]

    def fetch(expert):
        return (pltpu.make_async_copy(wgu_hbm.at[layer, expert], stage_gu, sems.at[0]),
                pltpu.make_async_copy(wdn_hbm.at[layer, expert], stage_dn, sems.at[1]))

    @pl.when(b == 0)
    def _():
        for cp in fetch(e):
            cp.start()

    @pl.when((b == 0) | (e != prev_e))
    def _():
        for cp in fetch(e):
            cp.wait()
        wgu_scr[...] = stage_gu[...].astype(BF16)
        wdn_scr[...] = stage_dn[...].astype(BF16)
        nxt = nx_ref[b]

        @pl.when(nxt >= 0)
        def _():
            for cp in fetch(nxt):
                cp.start()

    valid = bv_ref[b]
    for start in range(0, x_ref.shape[0], sub):
        rows = pl.ds(start, sub)

        @pl.when(valid > start)
        def _():
            x = x_ref[rows, :]
            row = lax.broadcasted_iota(I32, x.shape, 0) + start
            x = jnp.where(row < valid, x, 0.0).astype(BF16)
            hg = _dot(x, wgu_scr[...]) + bgu_ref[0, 0]
            g = jnp.minimum(hg[:, :d_ff], SWIGLU_LIMIT)
            lin = jnp.clip(hg[:, d_ff:], -SWIGLU_LIMIT, SWIGLU_LIMIT)
            act = g * _sigmoid(SWIGLU_ALPHA * g) * (lin + 1.0)
            y_ref[rows, :] = _dot(act.astype(BF16), wdn_scr[...]) + bdn_ref[0, 0]

        @pl.when(valid <= start)
        def _():
            y_ref[rows, :] = jnp.zeros((sub, y_ref.shape[1]), F32)


def _experts(xb, blk_e, blk_valid, blk_next_e, n_used, w_gu, b_gu, w_dn, b_dn, *, layer, bm):
    cap, d = xb.shape
    depth, n_e, _, two_ff = w_gu.shape
    d_ff = two_ff // 2
    nb = cap // bm
    grid_spec = pltpu.PrefetchScalarGridSpec(
        num_scalar_prefetch=4,
        grid=(nb,),
        in_specs=[pl.BlockSpec((bm, d), lambda b, be, bv, nx, nu: (jnp.minimum(b, nu[0] - 1), 0)),
                  pl.BlockSpec((1, 1, 1, two_ff), lambda b, be, bv, nx, nu: (layer, be[b], 0, 0)),
                  pl.BlockSpec((1, 1, 1, d), lambda b, be, bv, nx, nu: (layer, be[b], 0, 0)),
                  pl.BlockSpec(memory_space=pl.ANY), pl.BlockSpec(memory_space=pl.ANY)],
        out_specs=pl.BlockSpec((bm, d), lambda b, be, bv, nx, nu: (b, 0)),
        scratch_shapes=[pltpu.VMEM((d, two_ff), F32), pltpu.VMEM((d_ff, d), F32),
                        pltpu.VMEM((d, two_ff), BF16), pltpu.VMEM((d_ff, d), BF16), pltpu.SemaphoreType.DMA((2,))])
    return pl.pallas_call(
        functools.partial(_expert_kernel, d_ff=d_ff, sub=min(MOE_SUB_BLOCK, bm), layer=layer),
        out_shape=jax.ShapeDtypeStruct((cap, d), F32),
        grid_spec=grid_spec,
        compiler_params=_params("arbitrary"),
        name="moe_experts",
    )(blk_e, blk_valid, blk_next_e, n_used, xb, b_gu.reshape(depth, n_e, 1, two_ff), b_dn.reshape(depth, n_e, 1, d),
      w_gu, w_dn)


def _combine_kernel(sizes_ref, offs_ref, x_ref, gate_ref, pos_ref, tw_ref, yb_hbm, o_ref, ys_scr, sems,
                    *, rows, first_tile):
    tm = x_ref.shape[0]
    step, n_steps = pl.program_id(0), pl.num_programs(0)
    tile = step + first_tile
    slot = step % 2

    def run_copy(s):
        def make(loc, off, n):
            return pltpu.make_async_copy(yb_hbm.at[pl.ds(off, n)], ys_scr.at[s, pl.ds(loc, n)], sems.at[s])
        return make

    @pl.when(step == 0)
    def _():
        ys_scr[...] = jnp.zeros(ys_scr.shape, F32)
        _for_each_run(sizes_ref, offs_ref, tile, lambda *a: run_copy(slot)(*a).start())

    @pl.when(step + 1 < n_steps)
    def _():
        _for_each_run(sizes_ref, offs_ref, tile + 1, lambda *a: run_copy(1 - slot)(*a).start())

    _for_each_run(sizes_ref, offs_ref, tile, lambda *a: run_copy(slot)(*a).wait())

    r_iota = lax.broadcasted_iota(I32, (tm, rows), 1)
    wmat = jnp.zeros((tm, rows), F32)
    for k in range(TOP_K):
        wmat = wmat + jnp.where(r_iota == pos_ref[:, pl.ds(k, 1)], tw_ref[:, pl.ds(k, 1)], 0.0)
    y = _dot(wmat.astype(BF16), ys_scr[slot].astype(BF16))
    o_ref[...] = x_ref[...] + gate_ref[0] * y


def _combine(x, gate, pos, tw, yb, sizes, offs, *, tm, first_tile):
    t, d = x.shape
    ns, r, _ = gate.shape
    nt = t // tm
    tps = nt // ns
    rows = _moe_rows(tm)
    mod = pl.BlockSpec((1, r if r == 1 else tm, d),
                       lambda i, s, o: (i // tps, 0 if r == 1 else i % tps, 0))
    col4 = pl.BlockSpec((tm, TOP_K), lambda i, s, o: (i, 0))
    grid_spec = pltpu.PrefetchScalarGridSpec(
        num_scalar_prefetch=2,
        grid=(nt,),
        in_specs=[pl.BlockSpec((tm, d), lambda i, s, o: (i, 0)), mod, col4, col4,
                  pl.BlockSpec(memory_space=pl.ANY)],
        out_specs=pl.BlockSpec((tm, d), lambda i, s, o: (i, 0)),
        scratch_shapes=[pltpu.VMEM((2, rows, d), F32), pltpu.SemaphoreType.DMA((2,))])
    return pl.pallas_call(
        functools.partial(_combine_kernel, rows=rows, first_tile=first_tile),
        out_shape=jax.ShapeDtypeStruct((t, d), F32),
        grid_spec=grid_spec,
        compiler_params=_params("arbitrary"),
        name="moe_combine",
    )(sizes, offs, x, gate, pos, tw, yb)


def _round_up(x, m):
    return (x + m - 1) // m * m


def _moe_layer(groups, g, rw_t, rb, w_gu, b_gu, w_dn, b_dn, *, layer, bm):
    routed = [_router(x, g, sc, sh, rw_t, rb, tm=tm) for x, sc, sh, _, tm in groups]
    sizes = jnp.concatenate([r[3].reshape(-1, N_EXPERTS) for r in routed], axis=0)
    nt = sizes.shape[0]
    rows_e = jnp.sum(sizes, axis=0)
    region = _round_up(rows_e, bm)
    region_end = jnp.cumsum(region)
    region_base = region_end - region
    offs = region_base[None, :] + jnp.cumsum(sizes, axis=0) - sizes
    t_all = sum(x.shape[0] for x, *_ in groups)
    cap = _round_up(t_all * TOP_K + nt * N_EXPERTS * (MOE_ROW_ALIGN - 1), bm) + N_EXPERTS * bm
    nb = cap // bm
    blk_start = jnp.arange(nb, dtype=I32) * bm
    n_used = (region_end[-1] // bm).astype(I32).reshape(1)
    blk_e = jnp.sum((region_end[None, :] <= blk_start[:, None]).astype(I32), axis=1)
    last_e = jnp.sum((region_end < region_end[-1]).astype(I32))
    blk_e = jnp.where(blk_start < region_end[-1], blk_e, last_e).astype(I32)
    onehot = blk_e[:, None] == jnp.arange(N_EXPERTS, dtype=I32)[None, :]
    rows_end = jnp.sum(jnp.where(onehot, (region_base + rows_e)[None, :], 0), axis=1)
    blk_valid = jnp.clip(rows_end - blk_start, 0, bm).astype(I32)
    next_blk = jnp.sum(jnp.where(onehot, region_end[None, :], 0), axis=1) // bm
    next_onehot = next_blk[:, None] == jnp.arange(nb, dtype=I32)[None, :]
    blk_next_e = jnp.where(next_blk < n_used[0], jnp.sum(jnp.where(next_onehot, blk_e[None, :], 0), axis=1), -1)
    blk_next_e = blk_next_e.astype(I32)
    sizes_f = sizes.reshape(-1).astype(I32)
    offs_f = offs.reshape(-1).astype(I32)
    gaps = jnp.concatenate([region_base + rows_e, region - rows_e]).astype(I32)
    xb = _dispatch([(r[0], r[2], grp[4]) for r, grp in zip(routed, groups)], sizes_f, offs_f, gaps, n_used,
                   cap=cap, bm=bm)
    yb = _experts(xb, blk_e, blk_valid, blk_next_e, n_used, w_gu, b_gu, w_dn, b_dn, layer=layer, bm=bm)
    outs, first = [], 0
    for (x, _, _, gate, tm), (_, top_w, pos_t, _) in zip(groups, routed):
        outs.append(_combine(x, gate, pos_t.T, top_w.T, yb, sizes_f, offs_f, tm=tm, first_tile=first))
        first += x.shape[0] // tm
    return outs


def _hybrid_mixer(x, l, mod, pos, attend, sc_prev, p, *, b, s_len, prompt):
    t, d = x.shape
    tm = min(1024, s_len) if prompt else t
    sh1, sc1, g1 = mod
    li = l // 2
    lam_init = 0.8 - 0.6 * math.exp(-0.3 * l)
    qb, kr, kb, vr, vb, z = _in_proj(x, p["norm_mix_g"][l], sc1, sh1, p["hy_w_in"][li], _rope_tables(pos),
                                     p["q_norm_g"][li], p["k_norm_g"][li], tm=tm, tn=512)
    lams = (p["lambda_q1"][li], p["lambda_k1"][li], p["lambda_q2"][li], p["lambda_k2"][li])
    o = attend(qb, kb, vb, kr, vr, lams, p["subln_g"][li], li, lam_init)
    c_sc = p["sconv_w"].shape[-1]
    bcol, ccol, ucol = 0, 1, 2
    if prompt:
        o_sc, tail = _conv_prompt(z, ucol, sc_prev[li], p["sconv_w"][li], b=b, s_len=s_len, c=c_sc,
                                  tt=min(512, s_len), gate_cols=(ccol, bcol))
        sc_new = tail[:, SUBLANES - (SC_KERNEL - 1):]
    else:
        tmaj = lambda col: z[:, col * c_sc:(col + 1) * c_sc].reshape(b, s_len, c_sc).transpose(1, 0, 2)
        o_t, cu_t = _conv_sample(tmaj(ucol), sc_prev[li].transpose(1, 0, 2), p["sconv_w"][li],
                                 gates=(tmaj(ccol), tmaj(bcol)))
        o_sc = o_t.transpose(1, 0, 2).reshape(t, c_sc)
        sc_new = jnp.concatenate([sc_prev[li], cu_t.transpose(1, 0, 2)], axis=1)[:, -(SC_KERNEL - 1):]
    w_out = p["hy_w_out"][li]
    x = _matmul_residual([o, o_sc], [w_out[:d], w_out[d:]], None, x, g1, tm=tm, tn=512)
    n_heads = d // V_DIM
    return x, (kr.reshape(b, s_len, n_heads, V_DIM), vr.reshape(b, s_len, n_heads, V_DIM), sc_new)


def _conformer_mixer(x, l, mod, conf_prev, p, *, b, s_len, prompt):
    t, d = x.shape
    tm = min(1024, s_len) if prompt else t
    sh1, sc1, g1 = mod
    ci = l // 2
    u = _norm_mod_matmul(x, p["norm_mix_g"][l], sc1, sh1, p["conf_w1"][ci], p["conf_b1"][ci], tm=tm, tn=512, glu=True)
    ln_params = (p["conf_dw_b"][ci], p["conf_ln_g"][ci], p["conf_ln_b"][ci])
    if prompt:
        y = _conv_prompt(u, 0, conf_prev[ci], p["conf_dw"][ci], b=b, s_len=s_len, c=d, tt=min(512, s_len),
                         ln_params=ln_params)
        cf_new = u.reshape(b, s_len, d)[:, s_len - (CONF_KERNEL - 1):]
    else:
        u_t = u.reshape(b, s_len, d).transpose(1, 0, 2)
        y_t, _ = _conv_sample(u_t, conf_prev[ci].transpose(1, 0, 2), p["conf_dw"][ci], ln_params=ln_params)
        y = y_t.transpose(1, 0, 2).reshape(t, d)
        cf_new = jnp.concatenate([conf_prev[ci], u.reshape(b, s_len, d)], axis=1)[:, -(CONF_KERNEL - 1):]
    x = _matmul_residual([y], [p["conf_w2"][ci]], p["conf_b2"][ci], x, g1, tm=tm, tn=512)
    return x, (cf_new,)


def _mixer(x, l, mod, pos, attend, sc_prev, conf_prev, p, *, b, s_len, prompt):
    if l % 2 == 0:
        return _hybrid_mixer(x, l, mod, pos, attend, sc_prev, p, b=b, s_len=s_len, prompt=prompt)
    return _conformer_mixer(x, l, mod, conf_prev, p, b=b, s_len=s_len, prompt=prompt)


def kernel(x_prompt, x_sample, cache_k, cache_v, state_sconv, state_conf, page_table, c_prompt, c_sample,
           ada_w, ada_b, norm_mix_g, norm_ffn_g, hy_w_in, hy_w_out, q_norm_g, k_norm_g,
           lambda_q1, lambda_k1, lambda_q2, lambda_k2, subln_g, sconv_w,
           conf_w1, conf_b1, conf_dw, conf_dw_b, conf_ln_g, conf_ln_b, conf_w2, conf_b2,
           router_w, router_b, moe_w_gu, moe_b_gu, moe_w_dn, moe_b_dn):
    bp, s_len, d = x_prompt.shape
    bd, t_new, _ = x_sample.shape
    depth = ada_w.shape[0]
    p = {"ada_w": ada_w, "norm_mix_g": norm_mix_g, "norm_ffn_g": norm_ffn_g,
         "hy_w_in": hy_w_in.astype(BF16), "hy_w_out": hy_w_out.astype(BF16),
         "q_norm_g": q_norm_g, "k_norm_g": k_norm_g,
         "lambda_q1": lambda_q1, "lambda_k1": lambda_k1, "lambda_q2": lambda_q2, "lambda_k2": lambda_k2,
         "subln_g": subln_g, "sconv_w": sconv_w,
         "conf_w1": conf_w1.astype(BF16), "conf_b1": conf_b1, "conf_dw": conf_dw, "conf_dw_b": conf_dw_b,
         "conf_ln_g": conf_ln_g, "conf_ln_b": conf_ln_b, "conf_w2": conf_w2.astype(BF16), "conf_b2": conf_b2,
         "router_w_t": jnp.swapaxes(router_w, 1, 2), "router_b": router_b,
         "moe_w_gu": moe_w_gu, "moe_b_gu": moe_b_gu, "moe_w_dn": moe_w_dn, "moe_b_dn": moe_b_dn}

    mod_all = _ada_mod(jnp.concatenate([c_prompt, c_sample], axis=0), ada_w, ada_b)
    mods_p, mods_s = [], []
    for l in range(depth):
        parts = [mod_all[l, :, i * d:(i + 1) * d] for i in range(6)]
        mods_p.append([m[:bp].reshape(bp, 1, d) for m in parts])
        mods_s.append([jnp.repeat(m[bp:], t_new, axis=0).reshape(1, bd * t_new, d) for m in parts])

    past_len = page_table.shape[1] * PAGE_SIZE
    pos_prompt = jnp.arange(s_len, dtype=I32)
    pos_sample = jnp.tile(past_len + jnp.arange(t_new, dtype=I32), bd)

    def attend_prompt(qb, kb, vb, kr, vr, lams, sg, li, lam_init):
        return _flash_prompt(qb, kb, vb, lams, sg, b=bp, s_len=s_len, d=d, lam_init=lam_init,
                             tq=min(512, s_len), hb=4)

    def attend_sample(qb, kb, vb, kr, vr, lams, sg, li, lam_init):
        return _decode_attn(qb, kr, vr, cache_k, cache_v, li, page_table, lams, sg, bd=bd, t_new=t_new,
                            lam_init=lam_init, pages=math.gcd(DECODE_PAGES, page_table.shape[1]))

    n_even, n_odd = (depth + 1) // 2, depth // 2
    sc_zero = jnp.zeros((n_even, bp, SC_KERNEL - 1, sconv_w.shape[-1]), F32)
    conf_zero = jnp.zeros((n_odd, bp, CONF_KERNEL - 1, conf_dw.shape[-1]), F32)
    xp = x_prompt.reshape(bp * s_len, d)
    xs = x_sample.reshape(bd * t_new, d)
    new_p, new_s = [], []
    for l in range(depth):
        xp, st_p = _mixer(xp, l, mods_p[l][:3], pos_prompt, attend_prompt, sc_zero, conf_zero, p,
                          b=bp, s_len=s_len, prompt=True)
        xs, st_s = _mixer(xs, l, mods_s[l][:3], pos_sample, attend_sample, state_sconv, state_conf, p,
                          b=bd, s_len=t_new, prompt=False)
        new_p.append(st_p)
        new_s.append(st_s)
        (sh2p, sc2p, g2p), (sh2s, sc2s, g2s) = mods_p[l][3:], mods_s[l][3:]
        xp, xs = _moe_layer([(xp, sc2p, sh2p, g2p, min(256, bp * s_len)), (xs, sc2s, sh2s, g2s, bd * t_new)],
                            norm_ffn_g[l], p["router_w_t"][l], router_b[l],
                            moe_w_gu, moe_b_gu, moe_w_dn, moe_b_dn, layer=l, bm=MOE_BLOCK)

    def collect(new, b, s):
        even, odd = new[0::2], new[1::2]
        stack = lambda parts: parts[0][None] if len(parts) == 1 else jnp.stack(parts)
        return (stack([e[0] for e in even]), stack([e[1] for e in even]), stack([e[2] for e in even]),
                stack([o[0] for o in odd]))

    k_p, v_p, sc_p, cf_p = collect(new_p, bp, s_len)
    k_s, v_s, sc_s, cf_s = collect(new_s, bd, t_new)
    return (xp.reshape(bp, s_len, d), xs.reshape(bd, t_new, d), k_p, v_p, sc_p, cf_p, k_s, v_s, sc_s, cf_s)
```

```python
import functools
import math

import jax
import jax.numpy as jnp
from jax import lax
from jax.experimental import pallas as pl
from jax.experimental.pallas import tpu as pltpu

F32 = jnp.float32
BF16 = jnp.bfloat16
I32 = jnp.int32

EPS = 1e-6
HEAD_DIM = 64
V_DIM = 2 * HEAD_DIM
ROPE_DIM = HEAD_DIM // 4
ROPE_THETA = 500000.0
SC_KERNEL = 3
CONF_KERNEL = 31
N_EXPERTS = 32
TOP_K = 4
SWIGLU_LIMIT = 7.0
SWIGLU_ALPHA = 1.702
PAGE_SIZE = 128
DECODE_PAGES = 16
Q_SCALE = HEAD_DIM ** -0.5 * math.log2(math.e)

VMEM_LIMIT_BYTES = 56 * 1024 * 1024
SUBLANES = 8
MOE_ROW_ALIGN = 8
MOE_BLOCK = 512
MOE_SUB_BLOCK = 256


def _params(*sem):
    return pltpu.CompilerParams(dimension_semantics=sem, vmem_limit_bytes=VMEM_LIMIT_BYTES)


def _sigmoid(x):
    return 1.0 / (1.0 + jnp.exp(-x))


def _dot(a, b):
    return jnp.dot(a, b, preferred_element_type=F32)


def _dot_nt(a, b):
    return lax.dot_general(a, b, (((1,), (1,)), ((), ())), preferred_element_type=F32)


def _rmsnorm_mod(x, g, sc, sh):
    ms = jnp.mean(x * x, axis=-1, keepdims=True)
    return x * lax.rsqrt(ms + EPS) * g * (1.0 + sc) + sh


def _ada_kernel(c_ref, w_ref, b_ref, o_ref):
    c = c_ref[...]
    s = c * _sigmoid(c)
    o_ref[0] = jnp.dot(s, w_ref[0], precision=lax.Precision.HIGHEST, preferred_element_type=F32) + b_ref[0]


def _ada_mod(c_all, ada_w, ada_b):
    depth, d, n = ada_w.shape
    r = c_all.shape[0]
    tn = 1536
    return pl.pallas_call(
        _ada_kernel,
        out_shape=jax.ShapeDtypeStruct((depth, r, n), F32),
        grid=(depth, n // tn),
        in_specs=[pl.BlockSpec((r, d), lambda l, j: (0, 0)),
                  pl.BlockSpec((1, d, tn), lambda l, j: (l, 0, j)),
                  pl.BlockSpec((1, 1, tn), lambda l, j: (l, 0, j))],
        out_specs=pl.BlockSpec((1, r, tn), lambda l, j: (l, 0, j)),
        compiler_params=_params("arbitrary", "arbitrary"),
        name="ada_mod",
    )(c_all, ada_w, ada_b.reshape(depth, 1, n))


def _nmm_kernel(*refs, glu, has_bias):
    x_ref, g_ref, sc_ref, sh_ref = refs[:4]
    rest = list(refs[4:])
    w_refs = [rest.pop(0) for _ in range(2 if glu else 1)]
    b_refs = [rest.pop(0) for _ in range((2 if glu else 1) if has_bias else 0)]
    o_ref, h_scr = rest

    @pl.when(pl.program_id(1) == 0)
    def _():
        h = _rmsnorm_mod(x_ref[...], g_ref[...], sc_ref[0], sh_ref[0])
        h_scr[...] = h.astype(BF16)

    h = h_scr[...]
    accs = []
    for i, w_ref in enumerate(w_refs):
        a = _dot(h, w_ref[...])
        if has_bias:
            a = a + b_refs[i][...]
        accs.append(a)
    if glu:
        o_ref[...] = (accs[0] * _sigmoid(accs[1])).astype(o_ref.dtype)
    else:
        o_ref[...] = accs[0].astype(o_ref.dtype)


def _norm_mod_matmul(x, g, sc, sh, w, bias, *, tm, tn, glu=False, out_dtype=F32):
    t, d = x.shape
    n_out = w.shape[1] // 2 if glu else w.shape[1]
    ns, r, _ = sc.shape
    tps = (t // tm) // ns
    nj = n_out // tn
    in_specs = [pl.BlockSpec((tm, d), lambda i, j: (i, 0)),
                pl.BlockSpec((1, d), lambda i, j: (0, 0)),
                pl.BlockSpec((1, r, d), lambda i, j: (i // tps, 0, 0)),
                pl.BlockSpec((1, r, d), lambda i, j: (i // tps, 0, 0)),
                pl.BlockSpec((d, tn), lambda i, j: (0, j))]
    args = [x, g.reshape(1, d), sc, sh, w]
    if glu:
        in_specs.append(pl.BlockSpec((d, tn), lambda i, j: (0, j + nj)))
        args.append(w)
    if bias is not None:
        b2 = bias.reshape(1, -1)
        in_specs.append(pl.BlockSpec((1, tn), lambda i, j: (0, j)))
        args.append(b2)
        if glu:
            in_specs.append(pl.BlockSpec((1, tn), lambda i, j: (0, j + nj)))
            args.append(b2)
    return pl.pallas_call(
        functools.partial(_nmm_kernel, glu=glu, has_bias=bias is not None),
        out_shape=jax.ShapeDtypeStruct((t, n_out), out_dtype),
        grid=(t // tm, nj),
        in_specs=in_specs,
        out_specs=pl.BlockSpec((tm, tn), lambda i, j: (i, j)),
        scratch_shapes=[pltpu.VMEM((tm, d), BF16)],
        compiler_params=_params("arbitrary", "arbitrary"),
        name="norm_mod_matmul",
    )(*args)


def _group_mean_square(xh):
    r = lax.broadcasted_iota(I32, (V_DIM, V_DIM), 0) // HEAD_DIM
    c = lax.broadcasted_iota(I32, (V_DIM, V_DIM), 1) // HEAD_DIM
    ones_blk = jnp.where(r == c, 1.0, 0.0).astype(BF16)
    x2 = xh * xh
    hi = x2.astype(BF16)
    lo = (x2 - hi.astype(F32)).astype(BF16)
    return (_dot(hi, ones_blk) + _dot(lo, ones_blk)) * (1.0 / HEAD_DIM)


def _in_proj_kernel(x_ref, g_ref, sc_ref, sh_ref, w_ref, cos_ref, s1_ref, s2_ref, qg_ref, kg_ref,
                    qb_ref, krow_ref, kb_ref, vrow_ref, vb_ref, rest_ref, h_scr, *, steps_per_part):
    j = pl.program_id(1)
    spp = steps_per_part

    @pl.when(j == 0)
    def _():
        h_scr[...] = _rmsnorm_mod(x_ref[...], g_ref[...], sc_ref[0], sh_ref[0]).astype(BF16)

    acc = _dot(h_scr[...], w_ref[...])
    heads = acc.shape[1] // V_DIM

    def norm_rope(xh, g):
        y = xh * lax.rsqrt(_group_mean_square(xh) + EPS) * g
        return (y * cos_ref[...] + pltpu.roll(y, V_DIM - ROPE_DIM // 2, 1) * s1_ref[...]
                + pltpu.roll(y, ROPE_DIM // 2, 1) * s2_ref[...])

    @pl.when(j < spp)
    def _():
        for h in range(heads):
            sl = slice(h * V_DIM, (h + 1) * V_DIM)
            qb_ref[:, sl] = (norm_rope(acc[:, sl], qg_ref[...]) * Q_SCALE).astype(BF16)

    @pl.when((j >= spp) & (j < 2 * spp))
    def _():
        for h in range(heads):
            sl = slice(h * V_DIM, (h + 1) * V_DIM)
            kh = norm_rope(acc[:, sl], kg_ref[...])
            krow_ref[:, sl] = kh
            kb_ref[:, sl] = kh.astype(BF16)

    @pl.when((j >= 2 * spp) & (j < 3 * spp))
    def _():
        vrow_ref[...] = acc
        vb_ref[...] = acc.astype(BF16)

    @pl.when(j >= 3 * spp)
    def _():
        rest_ref[...] = acc


def _rope_tables(pos):
    half = ROPE_DIM // 2
    inv_freq = jnp.exp(-math.log(ROPE_THETA) * jnp.arange(half, dtype=F32) / half)
    ang = pos.astype(F32)[:, None] * inv_freq[None, :]
    cos, sin = jnp.cos(ang), jnp.sin(ang)
    p = pos.shape[0]
    ones = jnp.ones((p, HEAD_DIM - ROPE_DIM), F32)
    zeros = jnp.zeros((p, HEAD_DIM - ROPE_DIM), F32)
    z8 = jnp.zeros((p, half), F32)
    c64 = jnp.concatenate([cos, cos, ones], axis=1)
    s1_64 = jnp.concatenate([-sin, z8, zeros], axis=1)
    s2_64 = jnp.concatenate([z8, sin, zeros], axis=1)
    two = lambda a: jnp.concatenate([a, a], axis=1)
    return two(c64), two(s1_64), two(s2_64)


def _in_proj(x, g, sc, sh, w, tables, q_g, k_g, *, tm, tn):
    t, d = x.shape
    n_rest = w.shape[1] - 3 * d
    spp = d // tn
    nj = w.shape[1] // tn
    ns, r, _ = sc.shape
    tps = (t // tm) // ns
    npb = tables[0].shape[0] // tm
    part = lambda k: pl.BlockSpec((tm, tn), lambda i, j: (i, jnp.clip(j - k * spp, 0, spp - 1)))
    tab = pl.BlockSpec((tm, V_DIM), lambda i, j: (i % npb, 0))
    gspec = pl.BlockSpec((1, V_DIM), lambda i, j: (0, 0))
    mod = pl.BlockSpec((1, r, d), lambda i, j: (i // tps, 0, 0))
    g2 = lambda v: jnp.concatenate([v, v]).reshape(1, V_DIM)
    return pl.pallas_call(
        functools.partial(_in_proj_kernel, steps_per_part=spp),
        out_shape=[jax.ShapeDtypeStruct((t, d), BF16), jax.ShapeDtypeStruct((t, d), F32),
                   jax.ShapeDtypeStruct((t, d), BF16), jax.ShapeDtypeStruct((t, d), F32),
                   jax.ShapeDtypeStruct((t, d), BF16), jax.ShapeDtypeStruct((t, n_rest), F32)],
        grid=(t // tm, nj),
        in_specs=[pl.BlockSpec((tm, d), lambda i, j: (i, 0)), pl.BlockSpec((1, d), lambda i, j: (0, 0)), mod, mod,
                  pl.BlockSpec((d, tn), lambda i, j: (0, j)), tab, tab, tab, gspec, gspec],
        out_specs=[part(0), part(1), part(1), part(2), part(2),
                   pl.BlockSpec((tm, tn), lambda i, j: (i, jnp.clip(j - 3 * spp, 0, n_rest // tn - 1)))],
        scratch_shapes=[pltpu.VMEM((tm, d), BF16)],
        compiler_params=_params("arbitrary", "arbitrary"),
        name="in_proj",
    )(x, g.reshape(1, d), sc, sh, w, *tables, g2(q_g), g2(k_g))


def _diff_lambda(lq1, lk1, lq2, lk2, lam_init):
    a = jnp.sum(lq1 * lk1, axis=-1, keepdims=True)
    b = jnp.sum(lq2 * lk2, axis=-1, keepdims=True)
    return jnp.exp(a) - jnp.exp(b) + lam_init


def _subln(o, g, lam_init):
    ms = jnp.mean(o * o, axis=-1, keepdims=True)
    return o * lax.rsqrt(ms + EPS) * g * (1.0 - lam_init)


def _flash_kernel(q_ref, k_ref, v_ref, lq1, lk1, lq2, lk2, sg_ref, o_ref,
                  qm_scr, m_scr, acc_scr, *, lam_init, tq, hb):
    qi, ki, nk = pl.program_id(2), pl.program_id(3), pl.num_programs(3)

    @pl.when(ki == 0)
    def _():
        for h in range(hb):
            q = q_ref[:, h * V_DIM:(h + 1) * V_DIM]
            lane = lax.broadcasted_iota(I32, q.shape, 1)
            zero = jnp.zeros_like(q)
            qm_scr[h, pl.ds(0, tq), :] = jnp.where(lane < HEAD_DIM, q, zero)
            qm_scr[h, pl.ds(tq, tq), :] = jnp.where(lane >= HEAD_DIM, q, zero)
        m_scr[...] = jnp.full(m_scr.shape, -jnp.inf, F32)
        acc_scr[...] = jnp.zeros(acc_scr.shape, F32)

    def update(h, r0, nr, c0, nc, causal):
        rows, lanes = pl.ds(r0, nr), slice(h * V_DIM, (h + 1) * V_DIM)
        k = k_ref[pl.ds(c0, nc), lanes]
        v = v_ref[pl.ds(c0, nc), lanes]
        v_ext = jnp.concatenate([v, jnp.ones_like(v)], axis=1)
        s = _dot_nt(qm_scr[h, rows, :], k)
        if causal:
            row = (lax.broadcasted_iota(I32, s.shape, 0) + r0) % tq
            col = lax.broadcasted_iota(I32, s.shape, 1) + c0
            s = jnp.where(col <= row, s, -jnp.inf)
        m_old = m_scr[h, rows, :]
        m_new = jnp.maximum(m_old, jnp.max(s, axis=-1, keepdims=True))
        alpha = jnp.exp2(m_old - m_new)
        p = jnp.concatenate([jnp.exp2(s[:, j * V_DIM:(j + 1) * V_DIM] - m_new) for j in range(nc // V_DIM)], axis=1)
        acc_scr[h, rows, :] = (jnp.concatenate([alpha, alpha], axis=1) * acc_scr[h, rows, :]
                               + _dot(p.astype(BF16), v_ext))
        m_scr[h, rows, :] = m_new

    @pl.when(ki < qi)
    def _():
        for h in range(hb):
            update(h, 0, 2 * tq, 0, tq, False)

    @pl.when(ki == qi)
    def _():
        half = tq // 2
        for h in range(hb):
            update(h, 0, 2 * tq, 0, half, True)
            for c in range(2):
                update(h, c * tq + half, half, half, half, True)

    @pl.when(ki == nk - 1)
    def _():
        lam = _diff_lambda(lq1[...], lk1[...], lq2[...], lk2[...], lam_init)
        for h in range(hb):
            acc = acc_scr[h]
            o = acc[:tq, :V_DIM] / acc[:tq, V_DIM:] - lam * (acc[tq:, :V_DIM] / acc[tq:, V_DIM:])
            o_ref[:, h * V_DIM:(h + 1) * V_DIM] = _subln(o, sg_ref[...], lam_init).astype(o_ref.dtype)


def _flash_prompt(qb, kb, vb, lams, subln_g, *, b, s_len, d, lam_init, tq, hb):
    n_heads = d // V_DIM
    nq = s_len // tq
    w = hb * V_DIM
    lam_spec = pl.BlockSpec((1, HEAD_DIM), lambda bi, h, qi, ki: (0, 0))
    return pl.pallas_call(
        functools.partial(_flash_kernel, lam_init=lam_init, tq=tq, hb=hb),
        out_shape=jax.ShapeDtypeStruct((b * s_len, d), BF16),
        grid=(b, n_heads // hb, nq, nq),
        in_specs=[pl.BlockSpec((tq, w), lambda bi, h, qi, ki: (bi * nq + qi, h)),
                  pl.BlockSpec((tq, w), lambda bi, h, qi, ki: (bi * nq + jnp.minimum(ki, qi), h)),
                  pl.BlockSpec((tq, w), lambda bi, h, qi, ki: (bi * nq + jnp.minimum(ki, qi), h)),
                  lam_spec, lam_spec, lam_spec, lam_spec,
                  pl.BlockSpec((1, V_DIM), lambda bi, h, qi, ki: (0, 0))],
        out_specs=pl.BlockSpec((tq, w), lambda bi, h, qi, ki: (bi * nq + qi, h)),
        scratch_shapes=[pltpu.VMEM((hb, 2 * tq, V_DIM), BF16), pltpu.VMEM((hb, 2 * tq, V_DIM), F32),
                        pltpu.VMEM((hb, 2 * tq, 2 * V_DIM), F32)],
        compiler_params=_params("arbitrary", "arbitrary", "arbitrary", "arbitrary"),
        name="flash_prompt",
    )(qb, kb, vb, *[x.reshape(1, HEAD_DIM) for x in lams], subln_g.reshape(1, V_DIM))


def _decode_kernel(pt_ref, q_ref, kn_ref, vn_ref, lq1, lk1, lq2, lk2, sg_ref, *refs,
                   lam_init, n_heads, pages, nc, t_new, page_base):
    ck_hbm, cv_hbm, o_ref, kbuf, vbuf, sems, q_scr, m_scr, l_scr, acc_scr = refs
    bi = pl.program_id(0)
    n_steps = pl.num_programs(0) * nc
    rows = 2 * t_new * n_heads
    rows_per_page = PAGE_SIZE * n_heads
    row_head = lax.broadcasted_iota(I32, (rows, V_DIM), 0) // (2 * t_new)
    lane = lax.broadcasted_iota(I32, (rows, V_DIM), 1)

    def page_copies(g, slot):
        b, c = g // nc, g % nc
        copies = []
        for p in range(pages):
            page = page_base + pt_ref[b, c * pages + p]
            dst = pl.ds(p * rows_per_page, rows_per_page)
            copies.append(pltpu.make_async_copy(ck_hbm.at[page], kbuf.at[slot, dst], sems.at[slot, 0]))
            copies.append(pltpu.make_async_copy(cv_hbm.at[page], vbuf.at[slot, dst], sems.at[slot, 1]))
        return copies

    @pl.when(bi == 0)
    def _():
        for cp in page_copies(0, 0):
            cp.start()

    comp = (lax.broadcasted_iota(I32, (rows, V_DIM), 0) // t_new) % 2
    q = q_ref[0]
    q_scr[...] = jnp.where(lane // HEAD_DIM == comp, q, jnp.zeros_like(q))
    s = _dot_nt(q_scr[...], kn_ref[0].astype(BF16))
    r = lax.broadcasted_iota(I32, s.shape, 0)
    c = lax.broadcasted_iota(I32, s.shape, 1)
    visible = (c % n_heads == r // (2 * t_new)) & (c // n_heads <= r % t_new)
    s = jnp.where(visible, s, -jnp.inf)
    m = jnp.max(s, axis=-1, keepdims=True)
    p = jnp.exp2(s - m)
    m_scr[...] = jnp.broadcast_to(m, m_scr.shape)
    l_scr[...] = jnp.broadcast_to(jnp.sum(p, axis=-1, keepdims=True), l_scr.shape)
    acc_scr[...] = _dot(p.astype(BF16), vn_ref[0].astype(BF16))

    def chunk(ci, carry):
        step = bi * nc + ci
        slot = step % 2

        @pl.when(step + 1 < n_steps)
        def _():
            for cp in page_copies(step + 1, 1 - slot):
                cp.start()

        for cp in page_copies(step, slot):
            cp.wait()
        k2 = kbuf[slot].astype(BF16)
        v2 = vbuf[slot].astype(BF16)
        s = _dot_nt(q_scr[...], k2)
        bias = jnp.where(lane % n_heads == row_head, 0.0, -jnp.inf)
        pieces = [s[:, j * V_DIM:(j + 1) * V_DIM] + bias for j in range(s.shape[1] // V_DIM)]
        m_old = m_scr[...]
        m_new = jnp.maximum(m_old, jnp.max(functools.reduce(jnp.maximum, pieces), axis=-1, keepdims=True))
        alpha = jnp.exp2(m_old - m_new)
        ps = [jnp.exp2(piece - m_new) for piece in pieces]
        l_scr[...] = alpha * l_scr[...] + jnp.sum(functools.reduce(jnp.add, ps), axis=-1, keepdims=True)
        acc_scr[...] = alpha * acc_scr[...] + _dot(jnp.concatenate(ps, axis=1).astype(BF16), v2)
        m_scr[...] = m_new
        return carry

    lax.fori_loop(0, nc, chunk, 0)

    lam = _diff_lambda(lq1[...], lk1[...], lq2[...], lk2[...], lam_init)
    o = acc_scr[...] / l_scr[...]
    o = o - lam * pltpu.roll(o, rows - t_new, 0)
    o_ref[0] = _subln(o, sg_ref[...], lam_init)


def _decode_attn(qb, k_new, v_new, cache_k, cache_v, li, page_table, lams, subln_g, *, bd, t_new, lam_init, pages):
    d = qb.shape[1]
    n_heads = d // V_DIM
    n_pool = cache_k.shape[1]
    rows_per_page = PAGE_SIZE * n_heads
    ck = cache_k.reshape(-1, rows_per_page, V_DIM)
    cv = cache_v.reshape(-1, rows_per_page, V_DIM)
    n_pages = page_table.shape[1]
    nc = n_pages // pages
    rows = 2 * t_new * n_heads
    n_new = t_new * n_heads
    q2 = jnp.broadcast_to(qb.reshape(bd, t_new, n_heads, 1, V_DIM).transpose(0, 2, 3, 1, 4),
                          (bd, n_heads, 2, t_new, V_DIM)).reshape(bd, rows, V_DIM)

    new_spec = pl.BlockSpec((1, n_new, V_DIM), lambda b, pt: (b, 0, 0))
    lam_spec = pl.BlockSpec((1, HEAD_DIM), lambda b, pt: (0, 0))
    chunk_rows = pages * rows_per_page
    grid_spec = pltpu.PrefetchScalarGridSpec(
        num_scalar_prefetch=1,
        grid=(bd,),
        in_specs=[pl.BlockSpec((1, rows, V_DIM), lambda b, pt: (b, 0, 0)), new_spec, new_spec,
                  lam_spec, lam_spec, lam_spec, lam_spec, pl.BlockSpec((1, V_DIM), lambda b, pt: (0, 0)),
                  pl.BlockSpec(memory_space=pl.ANY), pl.BlockSpec(memory_space=pl.ANY)],
        out_specs=pl.BlockSpec((1, rows, V_DIM), lambda b, pt: (b, 0, 0)),
        scratch_shapes=[pltpu.VMEM((2, chunk_rows, V_DIM), F32), pltpu.VMEM((2, chunk_rows, V_DIM), F32),
                        pltpu.SemaphoreType.DMA((2, 2)),
                        pltpu.VMEM((rows, V_DIM), BF16), pltpu.VMEM((rows, V_DIM), F32),
                        pltpu.VMEM((rows, V_DIM), F32), pltpu.VMEM((rows, V_DIM), F32)])
    o2 = pl.pallas_call(
        functools.partial(_decode_kernel, lam_init=lam_init, n_heads=n_heads, pages=pages, nc=nc, t_new=t_new,
                          page_base=li * n_pool),
        out_shape=jax.ShapeDtypeStruct((bd, rows, V_DIM), F32),
        grid_spec=grid_spec,
        compiler_params=_params("arbitrary"),
        name="decode_attn",
    )(page_table, q2, k_new.reshape(bd, n_new, V_DIM), v_new.reshape(bd, n_new, V_DIM),
      *[x.reshape(1, HEAD_DIM) for x in lams], subln_g.reshape(1, V_DIM), ck, cv)
    o = o2.reshape(bd, n_heads, 2, t_new, V_DIM)[:, :, 0].transpose(0, 2, 1, 3)
    return o.reshape(bd * t_new, d).astype(BF16)


def _layernorm_silu(y, g, b):
    mu = jnp.mean(y, axis=-1, keepdims=True)
    yc = y - mu
    var = jnp.mean(yc * yc, axis=-1, keepdims=True)
    z = yc * lax.rsqrt(var + EPS) * g + b
    return z * _sigmoid(z)


CONV_ROW_CHUNK = 128
CONV_LANE_CHUNK = 128


def _depthwise_taps(ext, w_ref, y_scr, *, taps, base, tt):
    c = y_scr.shape[1]
    rc, cc = CONV_ROW_CHUNK, min(CONV_LANE_CHUNK, c)
    by_residue = {}
    for j in range(taps):
        by_residue.setdefault((base + j) % SUBLANES, []).append(j)

    def row_chunk(i, carry):
        row0 = pl.multiple_of(i * rc, rc)
        for c0 in range(0, c, cc):
            y = None
            for r, js in sorted(by_residue.items()):
                acc = None
                for j in js:
                    aligned = base + j - r
                    term = (w_ref[pl.ds(j, 1), pl.ds(c0, cc)]
                            * ext[pl.ds(row0 + aligned, rc + SUBLANES), pl.ds(c0, cc)])
                    acc = term if acc is None else acc + term
                part = acc[r:r + rc]
                y = part if y is None else y + part
            y_scr[pl.ds(row0, rc), pl.ds(c0, cc)] = y
        return carry

    lax.fori_loop(0, tt // rc, row_chunk, 0)


def _conv_prompt_kernel(*refs, taps, halo, tt, gated, ln):
    refs = list(refs)
    u_ref, uh_ref, prev_ref = refs[:3]
    del refs[:3]
    if gated:
        gc_ref, gch_ref, gb_ref = refs[:3]
        del refs[:3]
    w_ref = refs.pop(0)
    if ln:
        cb_ref, lg_ref, lb_ref = refs[:3]
        del refs[:3]
    o_ref = refs.pop(0)
    if gated:
        tail_ref = refs.pop(0)
    ext, y_scr = refs
    ti, nt = pl.program_id(1), pl.num_programs(1)

    cur = u_ref[...]
    if gated:
        cur = cur * gc_ref[...]
    ext[pl.ds(halo, tt), :] = cur

    @pl.when(ti == 0)
    def _():
        ext[pl.ds(0, halo), :] = prev_ref[0]

    @pl.when(ti > 0)
    def _():
        hv = uh_ref[...]
        if gated:
            hv = hv * gch_ref[...]
        ext[pl.ds(0, halo), :] = hv

    ext[pl.ds(halo + tt, SUBLANES), :] = jnp.zeros((SUBLANES, ext.shape[1]), F32)
    _depthwise_taps(ext, w_ref, y_scr, taps=taps, base=halo - (taps - 1), tt=tt)
    y = y_scr[...]
    if ln:
        y = _layernorm_silu(y + cb_ref[...], lg_ref[...], lb_ref[...])
    if gated:
        y = gb_ref[...] * y

        @pl.when(ti == nt - 1)
        def _():
            tail_ref[0] = ext[pl.ds(halo + tt - SUBLANES, SUBLANES), :]
    o_ref[...] = y.astype(o_ref.dtype)


def _conv_prompt(src, ucol, prev, w, *, b, s_len, c, tt, gate_cols=None, ln_params=None):
    taps = w.shape[0]
    halo = SUBLANES if taps - 1 <= SUBLANES else 32
    nt = s_len // tt
    hpt = tt // halo
    prev_p = jnp.pad(prev, ((0, 0), (halo - (taps - 1), 0), (0, 0)))
    cur = lambda col: pl.BlockSpec((tt, c), lambda bi, ti: (bi * nt + ti, col))
    hal = lambda col: pl.BlockSpec((halo, c), lambda bi, ti: (jnp.maximum((bi * nt + ti) * hpt - 1, 0), col))
    vec = pl.BlockSpec((1, c), lambda bi, ti: (0, 0))
    in_specs = [cur(ucol), hal(ucol), pl.BlockSpec((1, halo, c), lambda bi, ti: (bi, 0, 0))]
    args = [src, src, prev_p]
    gated = gate_cols is not None
    if gated:
        cc, cb = gate_cols
        in_specs += [cur(cc), hal(cc), cur(cb)]
        args += [src, src, src]
    in_specs.append(pl.BlockSpec((taps, c), lambda bi, ti: (0, 0)))
    args.append(w)
    if ln_params is not None:
        in_specs += [vec, vec, vec]
        args += [p.reshape(1, c) for p in ln_params]
    out_shape = [jax.ShapeDtypeStruct((b * s_len, c), BF16)]
    out_specs = [pl.BlockSpec((tt, c), lambda bi, ti: (bi * nt + ti, 0))]
    if gated:
        out_shape.append(jax.ShapeDtypeStruct((b, SUBLANES, c), F32))
        out_specs.append(pl.BlockSpec((1, SUBLANES, c), lambda bi, ti: (bi, 0, 0)))
    res = pl.pallas_call(
        functools.partial(_conv_prompt_kernel, taps=taps, halo=halo, tt=tt, gated=gated, ln=ln_params is not None),
        out_shape=out_shape,
        grid=(b, nt),
        in_specs=in_specs,
        out_specs=out_specs,
        scratch_shapes=[pltpu.VMEM((halo + tt + SUBLANES, c), F32), pltpu.VMEM((tt, c), F32)],
        compiler_params=_params("arbitrary", "arbitrary"),
        name="conv_prompt",
    )(*args)
    return res if gated else res[0]


def _conv_sample_kernel(*refs, taps, s_new, gated, ln):
    refs = list(refs)
    prev_ref, u_ref = refs[:2]
    del refs[:2]
    if gated:
        gc_ref, gb_ref = refs[:2]
        del refs[:2]
    w_ref = refs.pop(0)
    if ln:
        cb_ref, lg_ref, lb_ref = refs[:3]
        del refs[:3]
    o_ref, cu_ref, ext = refs
    for j in range(taps - 1):
        ext[j] = prev_ref[j]
    for s in range(s_new):
        cu = u_ref[s]
        if gated:
            cu = cu * gc_ref[s]
        cu_ref[s] = cu
        ext[taps - 1 + s] = cu
    for s in range(s_new):
        y = w_ref[0] * ext[s]
        for j in range(1, taps):
            y = y + w_ref[j] * ext[s + j]
        if ln:
            y = _layernorm_silu(y + cb_ref[...], lg_ref[...], lb_ref[...])
        if gated:
            y = gb_ref[s] * y
        o_ref[s] = y.astype(o_ref.dtype)


def _conv_sample(u, prev, w, *, gates=None, ln_params=None):
    s_new, bd, c = u.shape
    taps = w.shape[0]
    args = [prev, u]
    gated = gates is not None
    if gated:
        args += list(gates)
    args.append(w.reshape(taps, 1, c))
    if ln_params is not None:
        args += [p.reshape(1, c) for p in ln_params]
    return pl.pallas_call(
        functools.partial(_conv_sample_kernel, taps=taps, s_new=s_new, gated=gated, ln=ln_params is not None),
        out_shape=[jax.ShapeDtypeStruct((s_new, bd, c), BF16), jax.ShapeDtypeStruct((s_new, bd, c), F32)],
        scratch_shapes=[pltpu.VMEM((taps - 1 + s_new, bd, c), F32)],
        compiler_params=pltpu.CompilerParams(vmem_limit_bytes=VMEM_LIMIT_BYTES),
        name="conv_sample",
    )(*args)


def _mm_res_kernel(*refs, n_in, has_bias):
    a_refs, w_refs = refs[:n_in], refs[n_in:2 * n_in]
    rest = list(refs[2 * n_in:])
    b_ref = rest.pop(0) if has_bias else None
    x_ref, gate_ref, o_ref = rest
    acc = _dot(a_refs[0][...], w_refs[0][...])
    for a_ref, w_ref in zip(a_refs[1:], w_refs[1:]):
        acc = acc + _dot(a_ref[...], w_ref[...])
    if has_bias:
        acc = acc + b_ref[...]
    o_ref[...] = x_ref[...] + gate_ref[0] * acc


def _matmul_residual(a_list, w_list, bias, x, gate, *, tm, tn):
    t, d = x.shape
    ns, r, _ = gate.shape
    tps = (t // tm) // ns
    n_in = len(a_list)
    in_specs = [pl.BlockSpec((tm, a.shape[1]), lambda i, j: (i, 0)) for a in a_list]
    in_specs += [pl.BlockSpec((w.shape[0], tn), lambda i, j: (0, j)) for w in w_list]
    args = list(a_list) + list(w_list)
    if bias is not None:
        in_specs.append(pl.BlockSpec((1, tn), lambda i, j: (0, j)))
        args.append(bias.reshape(1, d))
    in_specs += [pl.BlockSpec((tm, tn), lambda i, j: (i, j)),
                 pl.BlockSpec((1, r, tn), lambda i, j: (i // tps, 0, j))]
    args += [x, gate]
    return pl.pallas_call(
        functools.partial(_mm_res_kernel, n_in=n_in, has_bias=bias is not None),
        out_shape=jax.ShapeDtypeStruct((t, d), F32),
        grid=(t // tm, d // tn),
        in_specs=in_specs,
        out_specs=pl.BlockSpec((tm, tn), lambda i, j: (i, j)),
        compiler_params=_params("arbitrary", "arbitrary"),
        name="matmul_residual",
    )(*args)


def _router_kernel(x_ref, g_ref, sc_ref, sh_ref, rw_ref, rb_ref,
                   hb_ref, tw_ref, pos_ref, sizes_ref, *, tm):
    h = _rmsnorm_mod(x_ref[...], g_ref[...], sc_ref[0], sh_ref[0])
    hb_ref[...] = h.astype(BF16)
    logits = lax.dot_general(rw_ref[...], h, (((1,), (1,)), ((), ())),
                             precision=lax.Precision.HIGHEST, preferred_element_type=F32) + rb_ref[...]
    e_iota = lax.broadcasted_iota(I32, logits.shape, 0)
    work = logits
    ids, vals = [], []
    multi_hot = jnp.zeros(logits.shape, F32)
    for _ in range(TOP_K):
        m = jnp.max(work, axis=0, keepdims=True)
        idx = jnp.min(jnp.where(work == m, e_iota, N_EXPERTS), axis=0, keepdims=True)
        sel = e_iota == idx
        work = jnp.where(sel, -jnp.inf, work)
        multi_hot = multi_hot + jnp.where(sel, 1.0, 0.0)
        ids.append(idx)
        vals.append(m)
    ex = [jnp.exp(v - vals[0]) for v in vals]
    den = ex[0] + ex[1] + ex[2] + ex[3]
    r = lax.broadcasted_iota(I32, (tm, tm), 0)
    c = lax.broadcasted_iota(I32, (tm, tm), 1)
    before = jnp.where(r < c, 1.0, 0.0).astype(BF16)
    earlier = _dot(multi_hot.astype(BF16), before)
    cnt = jnp.sum(multi_hot, axis=1, keepdims=True)
    sizes = jnp.ceil(cnt * (1.0 / MOE_ROW_ALIGN)) * MOE_ROW_ALIGN
    er = lax.broadcasted_iota(I32, (N_EXPERTS, N_EXPERTS), 0)
    ec = lax.broadcasted_iota(I32, (N_EXPERTS, N_EXPERTS), 1)
    lower = jnp.where(ec < er, 1.0, 0.0).astype(BF16)
    run_start = _dot(lower, jnp.broadcast_to(sizes, (N_EXPERTS, V_DIM)).astype(BF16))[:, :1]
    row_in_tile = run_start + earlier
    for k in range(TOP_K):
        tw_ref[pl.ds(k, 1), :] = ex[k] / den
        pos_ref[pl.ds(k, 1), :] = jnp.sum(jnp.where(e_iota == ids[k], row_in_tile, 0.0),
                                          axis=0, keepdims=True).astype(I32)
    sizes_ref[0] = sizes.astype(I32)


def _router(x, g, sc, sh, rw_t, rb, *, tm):
    t, d = x.shape
    ns, r, _ = sc.shape
    nt = t // tm
    tps = nt // ns
    mod = pl.BlockSpec((1, r if r == 1 else tm, d), lambda i: (i // tps, 0 if r == 1 else i % tps, 0))
    lane4 = pl.BlockSpec((TOP_K, tm), lambda i: (0, i))
    return pl.pallas_call(
        functools.partial(_router_kernel, tm=tm),
        out_shape=[jax.ShapeDtypeStruct((t, d), BF16), jax.ShapeDtypeStruct((TOP_K, t), F32),
                   jax.ShapeDtypeStruct((TOP_K, t), I32), jax.ShapeDtypeStruct((nt, N_EXPERTS, 1), I32)],
        grid=(nt,),
        in_specs=[pl.BlockSpec((tm, d), lambda i: (i, 0)), pl.BlockSpec((1, d), lambda i: (0, 0)), mod, mod,
                  pl.BlockSpec((N_EXPERTS, d), lambda i: (0, 0)), pl.BlockSpec((N_EXPERTS, 1), lambda i: (0, 0))],
        out_specs=[pl.BlockSpec((tm, d), lambda i: (i, 0)), lane4, lane4,
                   pl.BlockSpec((1, N_EXPERTS, 1), lambda i: (i, 0, 0))],
        compiler_params=_params("arbitrary"),
        name="moe_router",
    )(x, g.reshape(1, d), sc, sh, rw_t, rb.reshape(N_EXPERTS, 1))


def _for_each_run(sizes_ref, offs_ref, tile, action):
    def one(e, local):
        n = pl.multiple_of(sizes_ref[tile * N_EXPERTS + e], MOE_ROW_ALIGN)
        off = pl.multiple_of(offs_ref[tile * N_EXPERTS + e], MOE_ROW_ALIGN)
        loc = pl.multiple_of(local, MOE_ROW_ALIGN)

        @pl.when(n > 0)
        def _():
            action(loc, off, n)
        return local + n

    lax.fori_loop(0, N_EXPERTS, one, 0)


def _zero_fill_unused(gap_ref, nu_ref, xb_hbm, zero_scr, sem, *, bm, nb):
    zero_scr[...] = jnp.zeros(zero_scr.shape, F32)

    def gap_copy(e):
        n = pl.multiple_of(gap_ref[N_EXPERTS + e], MOE_ROW_ALIGN)
        off = pl.multiple_of(gap_ref[e], MOE_ROW_ALIGN)
        return n, pltpu.make_async_copy(zero_scr.at[pl.ds(0, n)], xb_hbm.at[pl.ds(off, n)], sem)

    def blk_copy(b):
        return pltpu.make_async_copy(zero_scr, xb_hbm.at[pl.ds(pl.multiple_of(b * bm, bm), bm)], sem)

    def each_gap(action):
        def body(e, carry):
            n, cp = gap_copy(e)

            @pl.when(n > 0)
            def _():
                action(cp)
            return carry
        lax.fori_loop(0, N_EXPERTS, body, 0)

    def each_blk(action):
        def body(b, carry):
            action(blk_copy(b))
            return carry
        lax.fori_loop(nu_ref[0], nb, body, 0)

    each_gap(lambda cp: cp.start())
    each_blk(lambda cp: cp.start())
    each_gap(lambda cp: cp.wait())
    each_blk(lambda cp: cp.wait())


def _moe_rows(tm):
    return tm * TOP_K + N_EXPERTS * MOE_ROW_ALIGN


def _dispatch_kernel(sizes_ref, offs_ref, gap_ref, nu_ref, *refs, tile_ranges, bm, nb):
    n_groups = len(tile_ranges)
    hb_refs, pos_refs = refs[0:2 * n_groups:2], refs[1:2 * n_groups:2]
    xb_hbm, sorted_scr, zero_scr, sems = refs[2 * n_groups:]
    tile, n_tiles = pl.program_id(0), pl.num_programs(0)
    slot = tile % 2

    for hb_ref, pos_ref, (first, last) in zip(hb_refs, pos_refs, tile_ranges):
        @pl.when((tile >= first) & (tile < last))
        def _():
            tm = hb_ref.shape[0]
            rows = _moe_rows(tm)
            r_iota = lax.broadcasted_iota(I32, (rows, tm), 0)
            hit = r_iota == pos_ref[pl.ds(0, 1), :]
            for k in range(1, TOP_K):
                hit = hit | (r_iota == pos_ref[pl.ds(k, 1), :])
            perm = jnp.where(hit, 1.0, 0.0).astype(BF16)
            sorted_scr[slot, pl.ds(0, rows), :] = _dot(perm, hb_ref[...])

    def run_copy(s):
        def make(loc, off, n):
            return pltpu.make_async_copy(sorted_scr.at[s, pl.ds(loc, n)], xb_hbm.at[pl.ds(off, n)], sems.at[s])
        return make

    _for_each_run(sizes_ref, offs_ref, tile, lambda *a: run_copy(slot)(*a).start())

    @pl.when(tile > 0)
    def _():
        _for_each_run(sizes_ref, offs_ref, tile - 1, lambda *a: run_copy(1 - slot)(*a).wait())

    @pl.when(tile == n_tiles - 1)
    def _():
        _for_each_run(sizes_ref, offs_ref, tile, lambda *a: run_copy(slot)(*a).wait())
        _zero_fill_unused(gap_ref, nu_ref, xb_hbm, zero_scr, sems.at[0], bm=bm, nb=nb)


def _dispatch(groups, sizes, offs, gaps, n_used, *, cap, bm):
    d = groups[0][0].shape[1]
    in_specs, args, tile_ranges = [], [], []
    first = 0
    for hb, pos_t, tm in groups:
        nt = hb.shape[0] // tm
        local = lambda i, first=first, nt=nt: jnp.clip(i - first, 0, nt - 1)
        in_specs += [pl.BlockSpec((tm, d), lambda i, *_, local=local: (local(i), 0)),
                     pl.BlockSpec((TOP_K, tm), lambda i, *_, local=local: (0, local(i)))]
        args += [hb, pos_t]
        tile_ranges.append((first, first + nt))
        first += nt
    rows = max(_moe_rows(tm) for _, _, tm in groups)
    grid_spec = pltpu.PrefetchScalarGridSpec(
        num_scalar_prefetch=4,
        grid=(first,),
        in_specs=in_specs,
        out_specs=pl.BlockSpec(memory_space=pl.ANY),
        scratch_shapes=[pltpu.VMEM((2, rows, d), F32), pltpu.VMEM((bm, d), F32), pltpu.SemaphoreType.DMA((2,))])
    return pl.pallas_call(
        functools.partial(_dispatch_kernel, tile_ranges=tuple(tile_ranges), bm=bm, nb=cap // bm),
        out_shape=jax.ShapeDtypeStruct((cap, d), F32),
        grid_spec=grid_spec,
        compiler_params=_params("arbitrary"),
        name="moe_dispatch",
    )(sizes, offs, gaps, n_used, *args)


def _expert_kernel(be_ref, bv_ref, nx_ref, nu_ref, x_ref, bgu_ref, bdn_ref, wgu_hbm, wdn_hbm, y_ref,
                   stage_gu, stage_dn, wgu_scr, wdn_scr, sems, *, d_ff, sub, layer):
    b = pl.program_id(0)
    e = be_ref[b]
    prev_e = be_ref[jnp.maximum(b - 1, 0)]

    def fetch(expert):
        return (pltpu.make_async_copy(wgu_hbm.at[layer, expert], stage_gu, sems.at[0]),
                pltpu.make_async_copy(wdn_hbm.at[layer, expert], stage_dn, sems.at[1]))

    @pl.when(b == 0)
    def _():
        for cp in fetch(e):
            cp.start()

    @pl.when((b == 0) | (e != prev_e))
    def _():
        for cp in fetch(e):
            cp.wait()
        wgu_scr[...] = stage_gu[...].astype(BF16)
        wdn_scr[...] = stage_dn[...].astype(BF16)
        nxt = nx_ref[b]

        @pl.when(nxt >= 0)
        def _():
            for cp in fetch(nxt):
                cp.start()

    valid = bv_ref[b]
    for start in range(0, x_ref.shape[0], sub):
        rows = pl.ds(start, sub)

        @pl.when(valid > start)
        def _():
            x = x_ref[rows, :]
            row = lax.broadcasted_iota(I32, x.shape, 0) + start
            x = jnp.where(row < valid, x, 0.0).astype(BF16)
            hg = _dot(x, wgu_scr[...]) + bgu_ref[0, 0]
            g = jnp.minimum(hg[:, :d_ff], SWIGLU_LIMIT)
            lin = jnp.clip(hg[:, d_ff:], -SWIGLU_LIMIT, SWIGLU_LIMIT)
            act = g * _sigmoid(SWIGLU_ALPHA * g) * (lin + 1.0)
            y_ref[rows, :] = _dot(act.astype(BF16), wdn_scr[...]) + bdn_ref[0, 0]

        @pl.when(valid <= start)
        def _():
            y_ref[rows, :] = jnp.zeros((sub, y_ref.shape[1]), F32)


def _experts(xb, blk_e, blk_valid, blk_next_e, n_used, w_gu, b_gu, w_dn, b_dn, *, layer, bm):
    cap, d = xb.shape
    depth, n_e, _, two_ff = w_gu.shape
    d_ff = two_ff // 2
    nb = cap // bm
    grid_spec = pltpu.PrefetchScalarGridSpec(
        num_scalar_prefetch=4,
        grid=(nb,),
        in_specs=[pl.BlockSpec((bm, d), lambda b, be, bv, nx, nu: (jnp.clip(b, 0, jnp.maximum(nu[0] - 1, 0)), 0)),
                  pl.BlockSpec((1, 1, 1, two_ff), lambda b, be, bv, nx, nu: (layer, be[b], 0, 0)),
                  pl.BlockSpec((1, 1, 1, d), lambda b, be, bv, nx, nu: (layer, be[b], 0, 0)),
                  pl.BlockSpec(memory_space=pl.ANY), pl.BlockSpec(memory_space=pl.ANY)],
        out_specs=pl.BlockSpec((bm, d), lambda b, be, bv, nx, nu: (b, 0)),
        scratch_shapes=[pltpu.VMEM((d, two_ff), F32), pltpu.VMEM((d_ff, d), F32),
                        pltpu.VMEM((d, two_ff), BF16), pltpu.VMEM((d_ff, d), BF16), pltpu.SemaphoreType.DMA((2,))])
    return pl.pallas_call(
        functools.partial(_expert_kernel, d_ff=d_ff, sub=min(MOE_SUB_BLOCK, bm), layer=layer),
        out_shape=jax.ShapeDtypeStruct((cap, d), F32),
        grid_spec=grid_spec,
        compiler_params=_params("arbitrary"),
        name="moe_experts",
    )(blk_e, blk_valid, blk_next_e, n_used, xb, b_gu.reshape(depth, n_e, 1, two_ff), b_dn.reshape(depth, n_e, 1, d),
      w_gu, w_dn)


def _combine_kernel(sizes_ref, offs_ref, x_ref, gate_ref, pos_ref, tw_ref, yb_hbm, o_ref, ys_scr, sems,
                    *, rows, first_tile):
    tm = x_ref.shape[0]
    step, n_steps = pl.program_id(0), pl.num_programs(0)
    tile = step + first_tile
    slot = step % 2

    def run_copy(s):
        def make(loc, off, n):
            return pltpu.make_async_copy(yb_hbm.at[pl.ds(off, n)], ys_scr.at[s, pl.ds(loc, n)], sems.at[s])
        return make

    @pl.when(step == 0)
    def _():
        ys_scr[...] = jnp.zeros(ys_scr.shape, F32)
        _for_each_run(sizes_ref, offs_ref, tile, lambda *a: run_copy(slot)(*a).start())

    @pl.when(step + 1 < n_steps)
    def _():
        _for_each_run(sizes_ref, offs_ref, tile + 1, lambda *a: run_copy(1 - slot)(*a).start())

    _for_each_run(sizes_ref, offs_ref, tile, lambda *a: run_copy(slot)(*a).wait())

    r_iota = lax.broadcasted_iota(I32, (tm, rows), 1)
    wmat = jnp.zeros((tm, rows), F32)
    for k in range(TOP_K):
        wmat = wmat + jnp.where(r_iota == pos_ref[:, pl.ds(k, 1)], tw_ref[:, pl.ds(k, 1)], 0.0)
    y = _dot(wmat.astype(BF16), ys_scr[slot].astype(BF16))
    o_ref[...] = x_ref[...] + gate_ref[0] * y


def _combine(x, gate, pos, tw, yb, sizes, offs, *, tm, first_tile):
    t, d = x.shape
    ns, r, _ = gate.shape
    nt = t // tm
    tps = nt // ns
    rows = _moe_rows(tm)
    mod = pl.BlockSpec((1, r if r == 1 else tm, d),
                       lambda i, s, o: (i // tps, 0 if r == 1 else i % tps, 0))
    col4 = pl.BlockSpec((tm, TOP_K), lambda i, s, o: (i, 0))
    grid_spec = pltpu.PrefetchScalarGridSpec(
        num_scalar_prefetch=2,
        grid=(nt,),
        in_specs=[pl.BlockSpec((tm, d), lambda i, s, o: (i, 0)), mod, col4, col4,
                  pl.BlockSpec(memory_space=pl.ANY)],
        out_specs=pl.BlockSpec((tm, d), lambda i, s, o: (i, 0)),
        scratch_shapes=[pltpu.VMEM((2, rows, d), F32), pltpu.SemaphoreType.DMA((2,))])
    return pl.pallas_call(
        functools.partial(_combine_kernel, rows=rows, first_tile=first_tile),
        out_shape=jax.ShapeDtypeStruct((t, d), F32),
        grid_spec=grid_spec,
        compiler_params=_params("arbitrary"),
        name="moe_combine",
    )(sizes, offs, x, gate, pos, tw, yb)


def _round_up(x, m):
    return (x + m - 1) // m * m


def _moe_layer(groups, g, rw_t, rb, w_gu, b_gu, w_dn, b_dn, *, layer, bm):
    routed = [_router(x, g, sc, sh, rw_t, rb, tm=tm) for x, sc, sh, _, tm in groups]
    sizes = jnp.concatenate([r[3].reshape(-1, N_EXPERTS) for r in routed], axis=0)
    nt = sizes.shape[0]
    rows_e = jnp.sum(sizes, axis=0)
    region = _round_up(rows_e, bm)
    region_end = jnp.cumsum(region)
    region_base = region_end - region
    offs = region_base[None, :] + jnp.cumsum(sizes, axis=0) - sizes
    t_all = sum(x.shape[0] for x, *_ in groups)
    cap = _round_up(t_all * TOP_K + nt * N_EXPERTS * (MOE_ROW_ALIGN - 1), bm) + N_EXPERTS * bm
    nb = cap // bm
    blk_start = jnp.arange(nb, dtype=I32) * bm
    n_used = (region_end[-1] // bm).astype(I32).reshape(1)
    blk_e = jnp.sum((region_end[None, :] <= blk_start[:, None]).astype(I32), axis=1)
    last_e = jnp.sum((region_end < region_end[-1]).astype(I32))
    blk_e = jnp.where(blk_start < region_end[-1], blk_e, last_e).astype(I32)
    onehot = blk_e[:, None] == jnp.arange(N_EXPERTS, dtype=I32)[None, :]
    rows_end = jnp.sum(jnp.where(onehot, (region_base + rows_e)[None, :], 0), axis=1)
    blk_valid = jnp.clip(rows_end - blk_start, 0, bm).astype(I32)
    next_blk = jnp.sum(jnp.where(onehot, region_end[None, :], 0), axis=1) // bm
    next_onehot = next_blk[:, None] == jnp.arange(nb, dtype=I32)[None, :]
    blk_next_e = jnp.where(next_blk < n_used[0], jnp.sum(jnp.where(next_onehot, blk_e[None, :], 0), axis=1), -1)
    blk_next_e = blk_next_e.astype(I32)
    sizes_f = sizes.reshape(-1).astype(I32)
    offs_f = offs.reshape(-1).astype(I32)
    gaps = jnp.concatenate([region_base + rows_e, region - rows_e]).astype(I32)
    xb = _dispatch([(r[0], r[2], grp[4]) for r, grp in zip(routed, groups)], sizes_f, offs_f, gaps, n_used,
                   cap=cap, bm=bm)
    yb = _experts(xb, blk_e, blk_valid, blk_next_e, n_used, w_gu, b_gu, w_dn, b_dn, layer=layer, bm=bm)
    outs, first = [], 0
    for (x, _, _, gate, tm), (_, top_w, pos_t, _) in zip(groups, routed):
        outs.append(_combine(x, gate, pos_t.T, top_w.T, yb, sizes_f, offs_f, tm=tm, first_tile=first))
        first += x.shape[0] // tm
    return outs


def _hybrid_mixer(x, l, mod, pos, attend, sc_prev, p, *, b, s_len, prompt):
    t, d = x.shape
    tm = min(1024, s_len) if prompt else t
    sh1, sc1, g1 = mod
    li = l // 2
    lam_init = 0.8 - 0.6 * math.exp(-0.3 * l)
    qb, kr, kb, vr, vb, z = _in_proj(x, p["norm_mix_g"][l], sc1, sh1, p["hy_w_in"][li], _rope_tables(pos),
                                     p["q_norm_g"][li], p["k_norm_g"][li], tm=tm, tn=512)
    lams = (p["lambda_q1"][li], p["lambda_k1"][li], p["lambda_q2"][li], p["lambda_k2"][li])
    o = attend(qb, kb, vb, kr, vr, lams, p["subln_g"][li], li, lam_init)
    c_sc = p["sconv_w"].shape[-1]
    bcol, ccol, ucol = 0, 1, 2
    if prompt:
        o_sc, tail = _conv_prompt(z, ucol, sc_prev[li], p["sconv_w"][li], b=b, s_len=s_len, c=c_sc,
                                  tt=min(512, s_len), gate_cols=(ccol, bcol))
        sc_new = tail[:, SUBLANES - (SC_KERNEL - 1):]
    else:
        tmaj = lambda col: z[:, col * c_sc:(col + 1) * c_sc].reshape(b, s_len, c_sc).transpose(1, 0, 2)
        o_t, cu_t = _conv_sample(tmaj(ucol), sc_prev[li].transpose(1, 0, 2), p["sconv_w"][li],
                                 gates=(tmaj(ccol), tmaj(bcol)))
        o_sc = o_t.transpose(1, 0, 2).reshape(t, c_sc)
        sc_new = jnp.concatenate([sc_prev[li], cu_t.transpose(1, 0, 2)], axis=1)[:, -(SC_KERNEL - 1):]
    w_out = p["hy_w_out"][li]
    x = _matmul_residual([o, o_sc], [w_out[:d], w_out[d:]], None, x, g1, tm=tm, tn=512)
    n_heads = d // V_DIM
    return x, (kr.reshape(b, s_len, n_heads, V_DIM), vr.reshape(b, s_len, n_heads, V_DIM), sc_new)


def _conformer_mixer(x, l, mod, conf_prev, p, *, b, s_len, prompt):
    t, d = x.shape
    tm = min(1024, s_len) if prompt else t
    sh1, sc1, g1 = mod
    ci = l // 2
    u = _norm_mod_matmul(x, p["norm_mix_g"][l], sc1, sh1, p["conf_w1"][ci], p["conf_b1"][ci], tm=tm, tn=512, glu=True)
    ln_params = (p["conf_dw_b"][ci], p["conf_ln_g"][ci], p["conf_ln_b"][ci])
    if prompt:
        y = _conv_prompt(u, 0, conf_prev[ci], p["conf_dw"][ci], b=b, s_len=s_len, c=d, tt=min(512, s_len),
                         ln_params=ln_params)
        cf_new = u.reshape(b, s_len, d)[:, s_len - (CONF_KERNEL - 1):]
    else:
        u_t = u.reshape(b, s_len, d).transpose(1, 0, 2)
        y_t, _ = _conv_sample(u_t, conf_prev[ci].transpose(1, 0, 2), p["conf_dw"][ci], ln_params=ln_params)
        y = y_t.transpose(1, 0, 2).reshape(t, d)
        cf_new = jnp.concatenate([conf_prev[ci], u.reshape(b, s_len, d)], axis=1)[:, -(CONF_KERNEL - 1):]
    x = _matmul_residual([y], [p["conf_w2"][ci]], p["conf_b2"][ci], x, g1, tm=tm, tn=512)
    return x, (cf_new,)


def _mixer(x, l, mod, pos, attend, sc_prev, conf_prev, p, *, b, s_len, prompt):
    if l % 2 == 0:
        return _hybrid_mixer(x, l, mod, pos, attend, sc_prev, p, b=b, s_len=s_len, prompt=prompt)
    return _conformer_mixer(x, l, mod, conf_prev, p, b=b, s_len=s_len, prompt=prompt)


def kernel(x_prompt, x_sample, cache_k, cache_v, state_sconv, state_conf, page_table, c_prompt, c_sample,
           ada_w, ada_b, norm_mix_g, norm_ffn_g, hy_w_in, hy_w_out, q_norm_g, k_norm_g,
           lambda_q1, lambda_k1, lambda_q2, lambda_k2, subln_g, sconv_w,
           conf_w1, conf_b1, conf_dw, conf_dw_b, conf_ln_g, conf_ln_b, conf_w2, conf_b2,
           router_w, router_b, moe_w_gu, moe_b_gu, moe_w_dn, moe_b_dn):
    bp, s_len, d = x_prompt.shape
    bd, t_new, _ = x_sample.shape
    depth = ada_w.shape[0]
    p = {"ada_w": ada_w, "norm_mix_g": norm_mix_g, "norm_ffn_g": norm_ffn_g,
         "hy_w_in": hy_w_in.astype(BF16), "hy_w_out": hy_w_out.astype(BF16),
         "q_norm_g": q_norm_g, "k_norm_g": k_norm_g,
         "lambda_q1": lambda_q1, "lambda_k1": lambda_k1, "lambda_q2": lambda_q2, "lambda_k2": lambda_k2,
         "subln_g": subln_g, "sconv_w": sconv_w,
         "conf_w1": conf_w1.astype(BF16), "conf_b1": conf_b1, "conf_dw": conf_dw, "conf_dw_b": conf_dw_b,
         "conf_ln_g": conf_ln_g, "conf_ln_b": conf_ln_b, "conf_w2": conf_w2.astype(BF16), "conf_b2": conf_b2,
         "router_w_t": jnp.swapaxes(router_w, 1, 2), "router_b": router_b,
         "moe_w_gu": moe_w_gu, "moe_b_gu": moe_b_gu, "moe_w_dn": moe_w_dn, "moe_b_dn": moe_b_dn}

    mod_all = _ada_mod(jnp.concatenate([c_prompt, c_sample], axis=0), ada_w, ada_b)
    mods_p, mods_s = [], []
    for l in range(depth):
        parts = [mod_all[l, :, i * d:(i + 1) * d] for i in range(6)]
        mods_p.append([m[:bp].reshape(bp, 1, d) for m in parts])
        mods_s.append([jnp.repeat(m[bp:], t_new, axis=0).reshape(1, bd * t_new, d) for m in parts])

    past_len = page_table.shape[1] * PAGE_SIZE
    pos_prompt = jnp.arange(s_len, dtype=I32)
    pos_sample = jnp.tile(past_len + jnp.arange(t_new, dtype=I32), bd)

    def attend_prompt(qb, kb, vb, kr, vr, lams, sg, li, lam_init):
        return _flash_prompt(qb, kb, vb, lams, sg, b=bp, s_len=s_len, d=d, lam_init=lam_init,
                             tq=min(512, s_len), hb=4)

    def attend_sample(qb, kb, vb, kr, vr, lams, sg, li, lam_init):
        return _decode_attn(qb, kr, vr, cache_k, cache_v, li, page_table, lams, sg, bd=bd, t_new=t_new,
                            lam_init=lam_init, pages=math.gcd(DECODE_PAGES, page_table.shape[1]))

    n_even, n_odd = (depth + 1) // 2, depth // 2
    sc_zero = jnp.zeros((n_even, bp, SC_KERNEL - 1, sconv_w.shape[-1]), F32)
    conf_zero = jnp.zeros((n_odd, bp, CONF_KERNEL - 1, conf_dw.shape[-1]), F32)
    xp = x_prompt.reshape(bp * s_len, d)
    xs = x_sample.reshape(bd * t_new, d)
    new_p, new_s = [], []
    for l in range(depth):
        xp, st_p = _mixer(xp, l, mods_p[l][:3], pos_prompt, attend_prompt, sc_zero, conf_zero, p,
                          b=bp, s_len=s_len, prompt=True)
        xs, st_s = _mixer(xs, l, mods_s[l][:3], pos_sample, attend_sample, state_sconv, state_conf, p,
                          b=bd, s_len=t_new, prompt=False)
        new_p.append(st_p)
        new_s.append(st_s)
        (sh2p, sc2p, g2p), (sh2s, sc2s, g2s) = mods_p[l][3:], mods_s[l][3:]
        xp, xs = _moe_layer([(xp, sc2p, sh2p, g2p, min(256, bp * s_len)), (xs, sc2s, sh2s, g2s, bd * t_new)],
                            norm_ffn_g[l], p["router_w_t"][l], router_b[l],
                            moe_w_gu, moe_b_gu, moe_w_dn, moe_b_dn, layer=l, bm=MOE_BLOCK)

    def collect(new, b, s):
        even, odd = new[0::2], new[1::2]
        stack = lambda parts: parts[0][None] if len(parts) == 1 else jnp.stack(parts)
        return (stack([e[0] for e in even]), stack([e[1] for e in even]), stack([e[2] for e in even]),
                stack([o[0] for o in odd]))

    k_p, v_p, sc_p, cf_p = collect(new_p, bp, s_len)
    k_s, v_s, sc_s, cf_s = collect(new_s, bd, t_new)
    return (xp.reshape(bp, s_len, d), xs.reshape(bd, t_new, d), k_p, v_p, sc_p, cf_p, k_s, v_s, sc_s, cf_s)
```
